```python
import math
import jax, jax.numpy as jnp
from jax import lax
import numpy as np

D_MODEL = 1024
BATCH = 4
SEQ = 4096
DEPTH = 1

PLE_DIM = 256
ROPE_THETA = 10000.0
Q_BLOCK = 128
LN_EPS = 1e-5
RMS_EPS = 1e-6
NEG_INF = -1e30
MAX_POS_OFFSET = 1024

DIFF_HEADS = 8
DIFF_HEAD_DIM = 64
DIFF_QK_WIDTH = 2 * DIFF_HEADS * DIFF_HEAD_DIM
DIFF_WIDTH = DIFF_HEADS * 2 * DIFF_HEAD_DIM

MLA_HEADS = 8
MLA_Q_LORA = 384
MLA_KV_LORA = 256
MLA_NOPE = 128
MLA_ROPE = 64
MLA_V = 128
MLA_WIDTH = MLA_HEADS * MLA_V

IN_SPLIT_SIZES = (DIFF_QK_WIDTH, DIFF_QK_WIDTH, DIFF_WIDTH, DIFF_WIDTH,
                  MLA_Q_LORA, MLA_KV_LORA, MLA_ROPE, MLA_WIDTH, 2 * D_MODEL)
N_IN = sum(IN_SPLIT_SIZES)

DEEPNORM_ALPHA = (2 * DEPTH) ** 0.25
DEEPNORM_BETA = (8 * DEPTH) ** -0.25

kernel_name = "diffattn_mla_gated_hybrid_deepnorm"


def layer_norm(x, g, b):
    xf = x.astype(jnp.float32)
    mu = jnp.mean(xf, -1, keepdims=True)
    var = jnp.mean(jnp.square(xf - mu), -1, keepdims=True)
    return ((xf - mu) * lax.rsqrt(var + LN_EPS) * g + b).astype(x.dtype)


def rms_norm(x, g):
    xf = x.astype(jnp.float32)
    y = xf * lax.rsqrt(jnp.mean(xf * xf, -1, keepdims=True) + RMS_EPS)
    return (y * g).astype(x.dtype)


def rope(x, positions):
    dim = x.shape[-1]
    inv = ROPE_THETA ** (-jnp.arange(0, dim, 2, dtype=jnp.float32) / dim)
    ang = positions.astype(jnp.float32)[:, None, :, None] * inv
    cos, sin = jnp.cos(ang), jnp.sin(ang)
    x1, x2 = jnp.split(x.astype(jnp.float32), 2, axis=-1)
    return jnp.concatenate([x1 * cos - x2 * sin, x2 * cos + x1 * sin], -1).astype(x.dtype)


def _query_blocks(t, seq_axis):
    nb = t.shape[seq_axis] // Q_BLOCK
    t = t.reshape(t.shape[:seq_axis] + (nb, Q_BLOCK) + t.shape[seq_axis + 1:])
    return jnp.moveaxis(t, seq_axis, 0)


def _causal_mask(block_idx, seq):
    q_pos = block_idx * Q_BLOCK + jnp.arange(Q_BLOCK)
    return jnp.arange(seq)[None, :] <= q_pos[:, None]


def diff_attention(q, k, v, lam):
    S = q.shape[3]
    scale = DIFF_HEAD_DIM ** -0.5
    qb = _query_blocks(q, 3)

    def one_block(args):
        q_blk, i = args
        s = jnp.einsum('bhcqd,bhckd->bhcqk', q_blk, k).astype(jnp.float32) * scale
        s = jnp.where(_causal_mask(i, S), s, NEG_INF)
        pr = jax.nn.softmax(s, axis=-1)
        w = (pr[:, :, 0] - lam * pr[:, :, 1]).astype(v.dtype)
        return jnp.einsum('bhqk,bhkd->bhqd', w, v)

    o = lax.map(one_block, (qb, jnp.arange(qb.shape[0])))
    _, B, H, _, dv = o.shape
    return jnp.moveaxis(o, 0, 2).reshape(B, H, S, dv)


def mla_attention(q_nope, q_pe, k_nope, k_pe, v):
    S = q_nope.shape[2]
    scale = (MLA_NOPE + MLA_ROPE) ** -0.5
    qn_b = _query_blocks(q_nope, 2)
    qp_b = _query_blocks(q_pe, 2)

    def one_block(args):
        qn, qp, i = args
        s = (jnp.einsum('bhqd,bhkd->bhqk', qn, k_nope)
             + jnp.einsum('bhqr,bkr->bhqk', qp, k_pe)).astype(jnp.float32) * scale
        s = jnp.where(_causal_mask(i, S), s, NEG_INF)
        pr = jax.nn.softmax(s, axis=-1).astype(v.dtype)
        return jnp.einsum('bhqk,bhkd->bhqd', pr, v)

    o = lax.map(one_block, (qn_b, qp_b, jnp.arange(qn_b.shape[0])))
    _, B, H, _, dv = o.shape
    return jnp.moveaxis(o, 0, 2).reshape(B, H, S, dv)


def setup_inputs(seed: int = 0) -> dict:
    key = jax.random.key(seed)
    ks = jax.random.split(key, 24)
    f32 = jnp.float32
    D = D_MODEL

    def nrm(k, shape, scale):
        return jax.random.normal(k, shape, f32) * scale

    x = nrm(ks[0], (BATCH, SEQ, D), 1.0)
    p = nrm(ks[1], (DEPTH, BATCH, SEQ, PLE_DIM), 1.0)
    positions = (jax.random.randint(ks[2], (BATCH, 1), 0, MAX_POS_OFFSET)
                 + jnp.arange(SEQ)[None, :]).astype(jnp.int32)
    return {
        "x": x,
        "p": p,
        "positions": positions,
        "ln_emb_g": 1.0 + nrm(ks[3], (D,), 0.02),
        "ln_emb_b": nrm(ks[4], (D,), 0.02),
        "w_in": nrm(ks[5], (DEPTH, D, N_IN), D ** -0.5),
        "b_gate": nrm(ks[6], (DEPTH, 2 * D), 0.02),
        "diff_lambda": nrm(ks[7], (DEPTH, 4, DIFF_HEAD_DIM), 0.1),
        "diff_subln_g": 1.0 + nrm(ks[8], (DEPTH, 2 * DIFF_HEAD_DIM), 0.02),
        "w_o_a": nrm(ks[9], (DEPTH, DIFF_WIDTH, D), DIFF_WIDTH ** -0.5 * DEEPNORM_BETA),
        "mla_q_norm_g": 1.0 + nrm(ks[10], (DEPTH, MLA_Q_LORA), 0.02),
        "mla_w_uq": nrm(ks[11], (DEPTH, MLA_Q_LORA, MLA_HEADS * (MLA_NOPE + MLA_ROPE)), MLA_Q_LORA ** -0.5),
        "mla_kv_norm_g": 1.0 + nrm(ks[12], (DEPTH, MLA_KV_LORA), 0.02),
        "mla_w_ukv": nrm(ks[13], (DEPTH, MLA_KV_LORA, MLA_HEADS * (MLA_NOPE + MLA_V)), MLA_KV_LORA ** -0.5),
        "w_o_b": nrm(ks[14], (DEPTH, MLA_WIDTH, D), MLA_WIDTH ** -0.5 * DEEPNORM_BETA),
        "w_out": nrm(ks[15], (DEPTH, D, D), D ** -0.5 * DEEPNORM_BETA),
        "ple_w_gate": nrm(ks[16], (DEPTH, D, D), D ** -0.5),
        "ple_b_gate": nrm(ks[17], (DEPTH, D), 0.02),
        "ple_w_proj": nrm(ks[18], (DEPTH, PLE_DIM, D), PLE_DIM ** -0.5 * DEEPNORM_BETA),
        "ln_post_g": 1.0 + nrm(ks[19], (DEPTH, D), 0.02),
        "ln_post_b": nrm(ks[20], (DEPTH, D), 0.02),
    }


def reference(x, p, positions, ln_emb_g, ln_emb_b, w_in, b_gate, diff_lambda, diff_subln_g,
              w_o_a, mla_q_norm_g, mla_w_uq, mla_kv_norm_g, mla_w_ukv, w_o_b, w_out,
              ple_w_gate, ple_b_gate, ple_w_proj, ln_post_g, ln_post_b):
    B, S, D = x.shape
    split_idx = [int(v) for v in np.cumsum(IN_SPLIT_SIZES)[:-1]]
    x = layer_norm(x, ln_emb_g, ln_emb_b)

    for i in range(DEPTH):
        h = x @ w_in[i]
        q_a, k_a, v_a, z_a, c_q, c_kv, k_pe, z_b, g_logit = jnp.split(h, split_idx, axis=-1)

        def heads2(t):
            t = t.reshape(B, S, 2 * DIFF_HEADS, DIFF_HEAD_DIM).transpose(0, 2, 1, 3)
            return rope(t, positions).reshape(B, DIFF_HEADS, 2, S, DIFF_HEAD_DIM)
        qa, ka = heads2(q_a), heads2(k_a)
        va = v_a.reshape(B, S, DIFF_HEADS, 2 * DIFF_HEAD_DIM).transpose(0, 2, 1, 3)
        lam_init = 0.8 - 0.6 * math.exp(-0.3 * i)
        lq = diff_lambda[i].astype(jnp.float32)
        lam = jnp.exp(jnp.sum(lq[0] * lq[1])) - jnp.exp(jnp.sum(lq[2] * lq[3])) + lam_init
        o_a = diff_attention(qa, ka, va, lam)
        o_a = rms_norm(o_a, diff_subln_g[i]) * (1.0 - lam_init)
        o_a = o_a.transpose(0, 2, 1, 3).reshape(B, S, DIFF_WIDTH)
        y_a = (o_a * jax.nn.silu(z_a)) @ w_o_a[i]

        qb = (rms_norm(c_q, mla_q_norm_g[i]) @ mla_w_uq[i])
        qb = qb.reshape(B, S, MLA_HEADS, MLA_NOPE + MLA_ROPE).transpose(0, 2, 1, 3)
        q_nope, q_pe = qb[..., :MLA_NOPE], rope(qb[..., MLA_NOPE:], positions)
        kv = (rms_norm(c_kv, mla_kv_norm_g[i]) @ mla_w_ukv[i])
        kv = kv.reshape(B, S, MLA_HEADS, MLA_NOPE + MLA_V).transpose(0, 2, 1, 3)
        k_nope, v_b = kv[..., :MLA_NOPE], kv[..., MLA_NOPE:]
        k_rot = rope(k_pe[:, None], positions)[:, 0]
        o_b = mla_attention(q_nope, q_pe, k_nope, k_rot, v_b)
        o_b = o_b.transpose(0, 2, 1, 3).reshape(B, S, MLA_WIDTH)
        y_b = (o_b * jax.nn.silu(z_b)) @ w_o_b[i]

        g_a, g_b = jnp.split(jax.nn.sigmoid(g_logit + b_gate[i]), 2, axis=-1)
        mix = (g_a * y_a + g_b * y_b) @ w_out[i]

        y = DEEPNORM_ALPHA * x + mix
        y = y + jax.nn.sigmoid(y @ ple_w_gate[i] + ple_b_gate[i]) * (p[i] @ ple_w_proj[i])
        x = layer_norm(y, ln_post_g[i], ln_post_b[i])

    return x
```

```python
import functools
import math

import jax
import jax.numpy as jnp
from jax import lax
from jax.experimental import pallas as pl
from jax.experimental.pallas import tpu as pltpu

F32 = jnp.float32
BF16 = jnp.bfloat16

D_MODEL = 1024
DEPTH = 1
PLE_DIM = 256
ROPE_THETA = 10000.0
LN_EPS = 1e-5
RMS_EPS = 1e-6
NEG_INF = -1e30

DIFF_HEADS = 8
DIFF_HEAD_DIM = 64
DIFF_QK_WIDTH = 2 * DIFF_HEADS * DIFF_HEAD_DIM
DIFF_WIDTH = DIFF_HEADS * 2 * DIFF_HEAD_DIM

MLA_HEADS = 8
MLA_Q_LORA = 384
MLA_KV_LORA = 256
MLA_NOPE = 128
MLA_ROPE = 64
MLA_V = 128
MLA_WIDTH = MLA_HEADS * MLA_V

DEEPNORM_ALPHA = (2 * DEPTH) ** 0.25
LAMBDA_INIT = 0.8 - 0.6 * math.exp(-0.3 * 0)

LANES = 128
HEAD_LANES = 2 * DIFF_HEAD_DIM
MLA_QK_PAD = 256
LOG2E = math.log2(math.e)

OFF_QA = 0
OFF_KA = OFF_QA + DIFF_QK_WIDTH
OFF_VA = OFF_KA + DIFF_QK_WIDTH
OFF_ZA = OFF_VA + DIFF_WIDTH
OFF_CQ = OFF_ZA + DIFF_WIDTH
OFF_CKV = OFF_CQ + MLA_Q_LORA
OFF_KPE = OFF_CKV + MLA_KV_LORA
OFF_ZB = OFF_KPE + LANES
OFF_G = OFF_ZB + MLA_WIDTH
N_IN_PAD = OFF_G + 2 * D_MODEL

VMEM_LIMIT_BYTES = 56 * 1024 * 1024


def _layer_norm(x, g, b):
    mu = jnp.mean(x, axis=-1, keepdims=True)
    xc = x - mu
    var = jnp.mean(xc * xc, axis=-1, keepdims=True)
    return xc * lax.rsqrt(var + LN_EPS) * g + b


def _rms_norm(x, g):
    return x * lax.rsqrt(jnp.mean(x * x, axis=-1, keepdims=True) + RMS_EPS) * g


def _dot(a, b):
    return jnp.dot(a, b, preferred_element_type=F32)


def _dot_nt(a, b):
    return lax.dot_general(a, b, (((1,), (1,)), ((), ())), preferred_element_type=F32)


def _proj_kernel(x_ref, pos_ref, inv_ref, lng_ref, lnb_ref, win_ref, bg_ref,
                 qng_ref, wuq_ref, kvg_ref, wukv_ref,
                 qa_ref, ka_ref, va_ref, za_ref, qb_ref, kb_ref, vb_ref, zb_ref, g_ref):
    xn = _layer_norm(x_ref[...], lng_ref[...], lnb_ref[...]).astype(BF16)

    ang = pos_ref[...].astype(F32) * inv_ref[...]
    cos = jnp.cos(ang)
    sin = jnp.sin(ang)
    lane = lax.broadcasted_iota(jnp.int32, (1, LANES), 1)
    first_half = (lane % DIFF_HEAD_DIM) < (DIFF_HEAD_DIM // 2)
    sin_signed = jnp.where(first_half, -sin, sin)

    def rope(t):
        partner = jnp.where(first_half, pltpu.roll(t, LANES - 32, 1), pltpu.roll(t, 32, 1))
        return t * cos + partner * sin_signed

    def proj(off, width):
        return _dot(xn, win_ref[:, off:off + width])

    qa_scale = DIFF_HEAD_DIM ** -0.5 * LOG2E
    hq = proj(OFF_QA, DIFF_QK_WIDTH)
    for h in range(DIFF_HEADS):
        qa_ref[h] = (rope(hq[:, h * HEAD_LANES:(h + 1) * HEAD_LANES]) * qa_scale).astype(BF16)
    hk = proj(OFF_KA, DIFF_QK_WIDTH)
    for h in range(DIFF_HEADS):
        ka_ref[h] = rope(hk[:, h * HEAD_LANES:(h + 1) * HEAD_LANES]).astype(BF16)
    hv = proj(OFF_VA, DIFF_WIDTH)
    for h in range(DIFF_HEADS):
        va_ref[h] = hv[:, h * HEAD_LANES:(h + 1) * HEAD_LANES].astype(BF16)
    hz = proj(OFF_ZA, DIFF_WIDTH)
    za_ref[...] = (hz * jax.nn.sigmoid(hz)).astype(BF16)

    qb_scale = (MLA_NOPE + MLA_ROPE) ** -0.5 * LOG2E
    cq = _rms_norm(proj(OFF_CQ, MLA_Q_LORA), qng_ref[...]).astype(BF16)
    uq = _dot(cq, wuq_ref[...])
    for h in range(MLA_HEADS):
        base = h * MLA_QK_PAD
        qb_ref[h, :, 0:LANES] = (uq[:, base:base + LANES] * qb_scale).astype(BF16)
        qb_ref[h, :, LANES:2 * LANES] = (rope(uq[:, base + LANES:base + 2 * LANES]) * qb_scale).astype(BF16)

    k_rot = rope(proj(OFF_KPE, LANES)).astype(BF16)
    ckv = _rms_norm(proj(OFF_CKV, MLA_KV_LORA), kvg_ref[...]).astype(BF16)
    ukv = _dot(ckv, wukv_ref[...])
    for h in range(MLA_HEADS):
        base = h * (MLA_NOPE + MLA_V)
        kb_ref[h, :, 0:LANES] = ukv[:, base:base + MLA_NOPE].astype(BF16)
        kb_ref[h, :, LANES:2 * LANES] = k_rot
        vb_ref[h] = ukv[:, base + MLA_NOPE:base + MLA_NOPE + MLA_V].astype(BF16)

    hzb = proj(OFF_ZB, MLA_WIDTH)
    zb_ref[...] = (hzb * jax.nn.sigmoid(hzb)).astype(BF16)
    g_ref[...] = jax.nn.sigmoid(proj(OFF_G, 2 * D_MODEL) + bg_ref[...]).astype(BF16)


def _resident(shape):
    nd = len(shape)
    return pl.BlockSpec(shape, lambda *_: (0,) * nd, pipeline_mode=pl.Buffered(1))


def _proj_call(x, pos, inv, lng, lnb, win, bg, qng, wuq, kvg, wukv, *, tm):
    B, S, D = x.shape
    grid = (B, S // tm)
    row = lambda b, i: (b, i, 0)
    head = lambda b, i: (b, 0, i, 0)
    bf = lambda shape: jax.ShapeDtypeStruct(shape, BF16)
    out_shape = [
        bf((B, DIFF_HEADS, S, HEAD_LANES)), bf((B, DIFF_HEADS, S, HEAD_LANES)),
        bf((B, DIFF_HEADS, S, HEAD_LANES)), bf((B, S, DIFF_WIDTH)),
        bf((B, MLA_HEADS, S, MLA_QK_PAD)), bf((B, MLA_HEADS, S, MLA_QK_PAD)),
        bf((B, MLA_HEADS, S, MLA_V)), bf((B, S, MLA_WIDTH)), bf((B, S, 2 * D)),
    ]
    out_specs = [
        pl.BlockSpec((None, DIFF_HEADS, tm, HEAD_LANES), head),
        pl.BlockSpec((None, DIFF_HEADS, tm, HEAD_LANES), head),
        pl.BlockSpec((None, DIFF_HEADS, tm, HEAD_LANES), head),
        pl.BlockSpec((None, tm, DIFF_WIDTH), row),
        pl.BlockSpec((None, MLA_HEADS, tm, MLA_QK_PAD), head),
        pl.BlockSpec((None, MLA_HEADS, tm, MLA_QK_PAD), head),
        pl.BlockSpec((None, MLA_HEADS, tm, MLA_V), head),
        pl.BlockSpec((None, tm, MLA_WIDTH), row),
        pl.BlockSpec((None, tm, 2 * D), row),
    ]
    in_specs = [
        pl.BlockSpec((None, tm, D), row),
        pl.BlockSpec((None, tm, 1), row),
        _resident(inv.shape), _resident(lng.shape), _resident(lnb.shape), _resident(win.shape),
        _resident(bg.shape), _resident(qng.shape), _resident(wuq.shape), _resident(kvg.shape),
        _resident(wukv.shape),
    ]
    return pl.pallas_call(
        _proj_kernel, grid=grid, in_specs=in_specs, out_specs=out_specs, out_shape=out_shape,
        compiler_params=pltpu.CompilerParams(
            dimension_semantics=("parallel", "parallel"), vmem_limit_bytes=VMEM_LIMIT_BYTES),
        name="proj",
    )(x, pos, inv, lng, lnb, win, bg, qng, wuq, kvg, wukv)


def _causal_keep(rows, tq, tk):
    r = lax.broadcasted_iota(jnp.int32, (rows, tk), 0) % tq
    c = lax.broadcasted_iota(jnp.int32, (rows, tk), 1)
    return c <= r


def _online_softmax_step(q, k, v, carry, keep):
    m_prev, l_prev, acc_prev = carry
    s = _dot_nt(q, k)
    if keep is not None:
        s = jnp.where(keep, s, NEG_INF)
    m_new = jnp.maximum(m_prev, jnp.max(s, axis=-1, keepdims=True))
    p = jnp.exp2(s - m_new)
    alpha = jnp.exp2(m_prev - m_new)
    l_new = alpha * l_prev + jnp.sum(p, axis=-1, keepdims=True)
    acc_new = alpha * acc_prev + _dot(p.astype(BF16), v)
    return m_new, l_new, acc_new


def _causal_attention(q, k_ref, v_ref, i, tq, dv):
    rows = q.shape[0]
    init = (jnp.full((rows, 1), NEG_INF, F32), jnp.zeros((rows, 1), F32), jnp.zeros((rows, dv), F32))

    def below_diagonal(j, carry):
        start = pl.multiple_of(j * tq, tq)
        return _online_softmax_step(q, k_ref[pl.ds(start, tq), :], v_ref[pl.ds(start, tq), :], carry, None)

    carry = lax.fori_loop(0, i, below_diagonal, init)
    start = pl.multiple_of(i * tq, tq)
    _, l, acc = _online_softmax_step(q, k_ref[pl.ds(start, tq), :], v_ref[pl.ds(start, tq), :], carry,
                                     _causal_keep(rows, tq, tq))
    return acc / l


def _diff_attn_kernel(q_ref, k_ref, v_ref, lam_ref, g_ref, o_ref, *, tq):
    i = pl.program_id(2)
    q = q_ref[...]
    lane = lax.broadcasted_iota(jnp.int32, (1, HEAD_LANES), 1)
    zero = jnp.zeros_like(q)
    qq = jnp.concatenate([jnp.where(lane < DIFF_HEAD_DIM, q, zero),
                          jnp.where(lane >= DIFF_HEAD_DIM, q, zero)], axis=0)
    o = _causal_attention(qq, k_ref, v_ref, i, tq, HEAD_LANES)

    lq = lam_ref[...]
    lam = (jnp.exp(jnp.sum(lq[0:1] * lq[1:2], axis=-1, keepdims=True))
           - jnp.exp(jnp.sum(lq[2:3] * lq[3:4], axis=-1, keepdims=True)) + LAMBDA_INIT)
    od = o[:tq] - lam * o[tq:]
    o_ref[...] = (_rms_norm(od, g_ref[...]) * (1.0 - LAMBDA_INIT)).astype(BF16)


def _diff_attn_call(qa, ka, va, lam, g, *, tq):
    B, H, S, _ = qa.shape
    qmap = lambda b, h, i: (b, h, i, 0)
    kvmap = lambda b, h, i: (b, h, 0, 0)
    return pl.pallas_call(
        functools.partial(_diff_attn_kernel, tq=tq),
        grid=(B, H, S // tq),
        in_specs=[
            pl.BlockSpec((None, None, tq, HEAD_LANES), qmap),
            pl.BlockSpec((None, None, S, HEAD_LANES), kvmap),
            pl.BlockSpec((None, None, S, HEAD_LANES), kvmap),
            pl.BlockSpec(lam.shape, lambda b, h, i: (0, 0)),
            pl.BlockSpec(g.shape, lambda b, h, i: (0, 0)),
        ],
        out_specs=pl.BlockSpec((None, None, tq, HEAD_LANES), qmap),
        out_shape=jax.ShapeDtypeStruct((B, H, S, HEAD_LANES), BF16),
        compiler_params=pltpu.CompilerParams(
            dimension_semantics=("parallel", "parallel", "arbitrary"), vmem_limit_bytes=VMEM_LIMIT_BYTES),
        name="diff_attn",
    )(qa, ka, va, lam, g)


def _mla_attn_kernel(q_ref, k_ref, v_ref, o_ref, *, tq):
    i = pl.program_id(2)
    o_ref[...] = _causal_attention(q_ref[...], k_ref, v_ref, i, tq, MLA_V).astype(BF16)


def _mla_attn_call(qb, kb, vb, *, tq):
    B, H, S, _ = qb.shape
    qmap = lambda b, h, i: (b, h, i, 0)
    kvmap = lambda b, h, i: (b, h, 0, 0)
    return pl.pallas_call(
        functools.partial(_mla_attn_kernel, tq=tq),
        grid=(B, H, S // tq),
        in_specs=[
            pl.BlockSpec((None, None, tq, MLA_QK_PAD), qmap),
            pl.BlockSpec((None, None, S, MLA_QK_PAD), kvmap),
            pl.BlockSpec((None, None, S, MLA_V), kvmap),
        ],
        out_specs=pl.BlockSpec((None, None, tq, MLA_V), qmap),
        out_shape=jax.ShapeDtypeStruct((B, H, S, MLA_V), BF16),
        compiler_params=pltpu.CompilerParams(
            dimension_semantics=("parallel", "parallel", "arbitrary"), vmem_limit_bytes=VMEM_LIMIT_BYTES),
        name="mla_attn",
    )(qb, kb, vb)


def _out_kernel(x_ref, lng_ref, lnb_ref, oa_ref, za_ref, ob_ref, zb_ref, g_ref, p_ref,
                woa_ref, wob_ref, wout_ref, wpg_ref, bpg_ref, wpp_ref, pog_ref, pob_ref, y_ref):
    xn = _layer_norm(x_ref[...], lng_ref[...], lnb_ref[...])
    oa = jnp.concatenate([oa_ref[h] for h in range(DIFF_HEADS)], axis=-1)
    ob = jnp.concatenate([ob_ref[h] for h in range(MLA_HEADS)], axis=-1)
    ya = _dot(oa * za_ref[...], woa_ref[...])
    yb = _dot(ob * zb_ref[...], wob_ref[...])
    g = g_ref[...]
    merged = g[:, :D_MODEL].astype(F32) * ya + g[:, D_MODEL:].astype(F32) * yb
    y = DEEPNORM_ALPHA * xn + _dot(merged.astype(BF16), wout_ref[...])
    gate = jax.nn.sigmoid(_dot(y.astype(BF16), wpg_ref[...]) + bpg_ref[...])
    y = y + gate * _dot(p_ref[...].astype(BF16), wpp_ref[...])
    y_ref[...] = _layer_norm(y, pog_ref[...], pob_ref[...])


def _out_call(x, lng, lnb, oa, za, ob, zb, g, p, woa, wob, wout, wpg, bpg, wpp, pog, pob, *, tm):
    B, S, D = x.shape
    row = lambda b, i: (b, i, 0)
    head = lambda b, i: (b, 0, i, 0)
    in_specs = [
        pl.BlockSpec((None, tm, D), row), _resident(lng.shape), _resident(lnb.shape),
        pl.BlockSpec((None, DIFF_HEADS, tm, HEAD_LANES), head),
        pl.BlockSpec((None, tm, DIFF_WIDTH), row),
        pl.BlockSpec((None, MLA_HEADS, tm, MLA_V), head),
        pl.BlockSpec((None, tm, MLA_WIDTH), row),
        pl.BlockSpec((None, tm, 2 * D), row),
        pl.BlockSpec((None, tm, PLE_DIM), row),
        _resident(woa.shape), _resident(wob.shape), _resident(wout.shape), _resident(wpg.shape),
        _resident(bpg.shape), _resident(wpp.shape), _resident(pog.shape), _resident(pob.shape),
    ]
    return pl.pallas_call(
        _out_kernel, grid=(B, S // tm), in_specs=in_specs,
        out_specs=pl.BlockSpec((None, tm, D), row),
        out_shape=jax.ShapeDtypeStruct((B, S, D), F32),
        compiler_params=pltpu.CompilerParams(
            dimension_semantics=("parallel", "parallel"), vmem_limit_bytes=VMEM_LIMIT_BYTES),
        name="merge_out",
    )(x, lng, lnb, oa, za, ob, zb, g, p, woa, wob, wout, wpg, bpg, wpp, pog, pob)


def _tiles(S):
    tm = min(256, S)
    tq = min(256, S)
    assert S % tm == 0 and S % tq == 0
    return tm, tq


def kernel(x, p, positions, ln_emb_g, ln_emb_b, w_in, b_gate, diff_lambda, diff_subln_g, w_o_a,
           mla_q_norm_g, mla_w_uq, mla_kv_norm_g, mla_w_ukv, w_o_b, w_out, ple_w_gate, ple_b_gate,
           ple_w_proj, ln_post_g, ln_post_b):
    B, S, D = x.shape
    assert D == D_MODEL and w_in.shape[0] == DEPTH == 1
    tm, tq = _tiles(S)
    row2 = lambda v: v.reshape(1, -1)

    w = w_in[0].astype(BF16)
    kpe_end = OFF_KPE + MLA_ROPE
    win = jnp.concatenate([w[:, :kpe_end], jnp.zeros((D, LANES - MLA_ROPE), BF16), w[:, kpe_end:]], axis=1)
    wuq = mla_w_uq[0].astype(BF16).reshape(MLA_Q_LORA, MLA_HEADS, MLA_NOPE + MLA_ROPE)
    wuq = jnp.pad(wuq, ((0, 0), (0, 0), (0, MLA_QK_PAD - MLA_NOPE - MLA_ROPE))).reshape(MLA_Q_LORA, -1)
    inv = ROPE_THETA ** (-jnp.arange(0, DIFF_HEAD_DIM, 2, dtype=F32) / DIFF_HEAD_DIM)
    inv = jnp.tile(inv, LANES // inv.shape[0]).reshape(1, LANES)

    qa, ka, va, za, qb, kb, vb, zb, g = _proj_call(
        x, positions.reshape(B, S, 1), inv, row2(ln_emb_g), row2(ln_emb_b), win, row2(b_gate[0]),
        row2(mla_q_norm_g[0]), wuq, row2(mla_kv_norm_g[0]), mla_w_ukv[0].astype(BF16), tm=tm)
    oa = _diff_attn_call(qa, ka, va, diff_lambda[0], row2(diff_subln_g[0]), tq=tq)
    ob = _mla_attn_call(qb, kb, vb, tq=tq)
    return _out_call(
        x, row2(ln_emb_g), row2(ln_emb_b), oa, za, ob, zb, g, p[0],
        w_o_a[0].astype(BF16), w_o_b[0].astype(BF16), w_out[0].astype(BF16), ple_w_gate[0].astype(BF16),
        row2(ple_b_gate[0]), ple_w_proj[0].astype(BF16), row2(ln_post_g[0]), row2(ln_post_b[0]), tm=tm)
```

```python
import functools
import math

import jax
import jax.numpy as jnp
from jax import lax
from jax.experimental import pallas as pl
from jax.experimental.pallas import tpu as pltpu

F32 = jnp.float32
BF16 = jnp.bfloat16

D_MODEL = 1024
DEPTH = 1
PLE_DIM = 256
ROPE_THETA = 10000.0
LN_EPS = 1e-5
RMS_EPS = 1e-6
NEG_INF = -1e30

DIFF_HEADS = 8
DIFF_HEAD_DIM = 64
DIFF_QK_WIDTH = 2 * DIFF_HEADS * DIFF_HEAD_DIM
DIFF_WIDTH = DIFF_HEADS * 2 * DIFF_HEAD_DIM

MLA_HEADS = 8
MLA_Q_LORA = 384
MLA_KV_LORA = 256
MLA_NOPE = 128
MLA_ROPE = 64
MLA_V = 128
MLA_WIDTH = MLA_HEADS * MLA_V

DEEPNORM_ALPHA = (2 * DEPTH) ** 0.25
LAMBDA_INIT = 0.8 - 0.6 * math.exp(-0.3 * 0)

LANES = 128
HEAD_LANES = 2 * DIFF_HEAD_DIM
MLA_QK_PAD = 256
LOG2E = math.log2(math.e)
HALF_ROT = DIFF_HEAD_DIM // 2

KEY_CHUNK = 256
ATTN_COLUMNS = 1024

OFF_KA = 0
OFF_ZA = OFF_KA + DIFF_QK_WIDTH
OFF_CQ = OFF_ZA + DIFF_WIDTH
OFF_CKV = OFF_CQ + MLA_Q_LORA
OFF_KPE = OFF_CKV + MLA_KV_LORA
OFF_ZB = OFF_KPE + LANES
OFF_G = OFF_ZB + MLA_WIDTH

VMEM_LIMIT_BYTES = 56 * 1024 * 1024


def _layer_norm(x, g, b):
    mu = jnp.mean(x, axis=-1, keepdims=True)
    xc = x - mu
    var = jnp.mean(xc * xc, axis=-1, keepdims=True)
    return xc * lax.rsqrt(var + LN_EPS) * g + b


def _rms_norm(x, g):
    return x * lax.rsqrt(jnp.mean(x * x, axis=-1, keepdims=True) + RMS_EPS) * g


def _dot(a, b):
    return jnp.dot(a, b, preferred_element_type=F32)


def _dot_nt(a, b):
    return lax.dot_general(a, b, (((1,), (1,)), ((), ())), preferred_element_type=F32)


def _proj_kernel(x_ref, pos_ref, inv_ref, lng_ref, lnb_ref, win_ref, wqat_ref, wvat_ref, bg_ref,
                 qng_ref, wuqt_ref, kvg_ref, wuk_ref, wuvt_ref,
                 qat_ref, ka_ref, vat_ref, za_ref, qbt_ref, kb_ref, vbt_ref, zb_ref, g_ref):
    tm = x_ref.shape[0]
    tk = qat_ref.shape[-1]
    xn = _layer_norm(x_ref[...], lng_ref[...], lnb_ref[...]).astype(BF16)

    ang = pos_ref[...].astype(F32) * inv_ref[...]
    cos = jnp.cos(ang)
    sin = jnp.sin(ang)
    lane = lax.broadcasted_iota(jnp.int32, (1, LANES), 1)
    first_half = (lane % DIFF_HEAD_DIM) < HALF_ROT
    sin_signed = jnp.where(first_half, -sin, sin)
    cos_t = cos.T
    sin_signed_t = sin_signed.T

    def rope(t):
        partner = jnp.where(first_half, pltpu.roll(t, LANES - HALF_ROT, 1), pltpu.roll(t, HALF_ROT, 1))
        return t * cos + partner * sin_signed

    def rope_t(t):
        h = HALF_ROT
        partner = jnp.concatenate([t[h:2 * h], t[0:h], t[3 * h:4 * h], t[2 * h:3 * h]], axis=0)
        return t * cos_t + partner * sin_signed_t

    def proj(off, width):
        return _dot(xn, win_ref[:, off:off + width])

    def store_chunks(ref, h, rows, t):
        for c in range(tm // tk):
            ref[h, c, rows, :] = t[:, c * tk:(c + 1) * tk].astype(BF16)

    all_rows = slice(None)
    qa_scale = DIFF_HEAD_DIM ** -0.5 * LOG2E
    hqt = _dot_nt(wqat_ref[...], xn)
    for h in range(DIFF_HEADS):
        store_chunks(qat_ref, h, all_rows, rope_t(hqt[h * HEAD_LANES:(h + 1) * HEAD_LANES]) * qa_scale)
    hk = proj(OFF_KA, DIFF_QK_WIDTH)
    for h in range(DIFF_HEADS):
        ka_ref[h] = rope(hk[:, h * HEAD_LANES:(h + 1) * HEAD_LANES]).astype(BF16)
    hvt = _dot_nt(wvat_ref[...], xn)
    for h in range(DIFF_HEADS):
        store_chunks(vat_ref, h, all_rows, hvt[h * HEAD_LANES:(h + 1) * HEAD_LANES])
    hz = proj(OFF_ZA, DIFF_WIDTH)
    za_ref[...] = (hz * jax.nn.sigmoid(hz)).astype(BF16)

    qb_scale = (MLA_NOPE + MLA_ROPE) ** -0.5 * LOG2E
    cq = _rms_norm(proj(OFF_CQ, MLA_Q_LORA), qng_ref[...]).astype(BF16)
    uqt = _dot_nt(wuqt_ref[...], cq)
    for h in range(MLA_HEADS):
        base = h * MLA_QK_PAD
        store_chunks(qbt_ref, h, slice(0, LANES), uqt[base:base + LANES] * qb_scale)
        store_chunks(qbt_ref, h, slice(LANES, 2 * LANES), rope_t(uqt[base + LANES:base + 2 * LANES]) * qb_scale)

    k_rot = rope(proj(OFF_KPE, LANES)).astype(BF16)
    ckv = _rms_norm(proj(OFF_CKV, MLA_KV_LORA), kvg_ref[...]).astype(BF16)
    uk = _dot(ckv, wuk_ref[...])
    for h in range(MLA_HEADS):
        kb_ref[h, :, 0:LANES] = uk[:, h * MLA_NOPE:(h + 1) * MLA_NOPE].astype(BF16)
        kb_ref[h, :, LANES:2 * LANES] = k_rot
    uvt = _dot_nt(wuvt_ref[...], ckv)
    for h in range(MLA_HEADS):
        store_chunks(vbt_ref, h, all_rows, uvt[h * MLA_V:(h + 1) * MLA_V])

    hzb = proj(OFF_ZB, MLA_WIDTH)
    zb_ref[...] = (hzb * jax.nn.sigmoid(hzb)).astype(BF16)
    g_ref[...] = jax.nn.sigmoid(proj(OFF_G, 2 * D_MODEL) + bg_ref[...]).astype(BF16)


def _resident(shape):
    nd = len(shape)
    return pl.BlockSpec(shape, lambda *_: (0,) * nd, pipeline_mode=pl.Buffered(1))


def _proj_call(x, pos, consts, *, tm, tk):
    B, S, D = x.shape
    grid = (B, S // tm)
    row = lambda b, i: (b, i, 0)
    head = lambda b, i: (b, 0, i, 0)
    head_t = lambda b, i: (b, 0, i, 0, 0)
    bf = lambda shape: jax.ShapeDtypeStruct(shape, BF16)
    nc = S // tk
    out_shape = [
        bf((B, DIFF_HEADS, nc, HEAD_LANES, tk)), bf((B, DIFF_HEADS, S, HEAD_LANES)),
        bf((B, DIFF_HEADS, nc, HEAD_LANES, tk)), bf((B, S, DIFF_WIDTH)),
        bf((B, MLA_HEADS, nc, MLA_QK_PAD, tk)), bf((B, MLA_HEADS, S, MLA_QK_PAD)),
        bf((B, MLA_HEADS, nc, MLA_V, tk)), bf((B, S, MLA_WIDTH)), bf((B, S, 2 * D)),
    ]
    out_specs = [
        pl.BlockSpec((None, DIFF_HEADS, tm // tk, HEAD_LANES, tk), head_t),
        pl.BlockSpec((None, DIFF_HEADS, tm, HEAD_LANES), head),
        pl.BlockSpec((None, DIFF_HEADS, tm // tk, HEAD_LANES, tk), head_t),
        pl.BlockSpec((None, tm, DIFF_WIDTH), row),
        pl.BlockSpec((None, MLA_HEADS, tm // tk, MLA_QK_PAD, tk), head_t),
        pl.BlockSpec((None, MLA_HEADS, tm, MLA_QK_PAD), head),
        pl.BlockSpec((None, MLA_HEADS, tm // tk, MLA_V, tk), head_t),
        pl.BlockSpec((None, tm, MLA_WIDTH), row),
        pl.BlockSpec((None, tm, 2 * D), row),
    ]
    in_specs = [pl.BlockSpec((None, tm, D), row), pl.BlockSpec((None, tm, 1), row)]
    in_specs += [_resident(c.shape) for c in consts]
    return pl.pallas_call(
        _proj_kernel, grid=grid, in_specs=in_specs, out_specs=out_specs, out_shape=out_shape,
        compiler_params=pltpu.CompilerParams(
            dimension_semantics=("parallel", "parallel"), vmem_limit_bytes=VMEM_LIMIT_BYTES),
        name="proj",
    )(x, pos, *consts)


def _attention_step(qt, k, vt, m_prev, l_prev, acc_ref, c0, keep):
    st = _dot(k, qt)
    if keep is not None:
        st = jnp.where(keep, st, NEG_INF)
    m_new = jnp.maximum(m_prev, jnp.max(st, axis=0, keepdims=True))
    pt = jnp.exp2(st - m_new)
    alpha = jnp.exp2(m_prev - m_new)
    l_new = alpha * l_prev + jnp.sum(pt, axis=0, keepdims=True)
    acc_ref[:, c0:] = alpha * acc_ref[:, c0:] + _dot(vt, pt.astype(BF16))
    return m_new, l_new


def _causal_attention(qt, k_ref, vt_ref, acc_ref, i, tk, maps):
    nq = qt.shape[1]
    group = maps * tk
    subtiles = nq // group
    acc_ref[...] = jnp.zeros_like(acc_ref)
    m = jnp.full((1, nq), NEG_INF, F32)
    l = jnp.zeros((1, nq), F32)

    def chunk(j):
        start = pl.multiple_of(j * tk, tk)
        return k_ref[pl.ds(start, tk), :], vt_ref[j]

    def below_diagonal(j, carry):
        return _attention_step(qt, *chunk(j), *carry, acc_ref, 0, None)

    m, l = lax.fori_loop(0, i * subtiles, below_diagonal, (m, l))
    for r in range(subtiles):
        c0 = r * group
        col = c0 + lax.broadcasted_iota(jnp.int32, (tk, nq - c0), 1)
        query = (col // group) * tk + col % tk
        key = r * tk + lax.broadcasted_iota(jnp.int32, (tk, nq - c0), 0)
        m_new, l_new = _attention_step(qt[:, c0:], *chunk(i * subtiles + r), m[:, c0:], l[:, c0:],
                                       acc_ref, c0, key <= query)
        m = jnp.concatenate([m[:, :c0], m_new], axis=1) if c0 else m_new
        l = jnp.concatenate([l[:, :c0], l_new], axis=1) if c0 else l_new
    return acc_ref[...] / l


def _diff_attn_kernel(qt_ref, k_ref, vt_ref, lam_ref, g_ref, o_ref, acc_ref):
    i = pl.program_id(2)
    subtiles, dk, tk = qt_ref.shape
    feature = lax.broadcasted_iota(jnp.int32, (dk, 1), 0)
    cols = []
    for s in range(subtiles):
        q = qt_ref[s]
        zero = jnp.zeros_like(q)
        cols += [jnp.where(feature < DIFF_HEAD_DIM, q, zero), jnp.where(feature >= DIFF_HEAD_DIM, q, zero)]
    ot = _causal_attention(jnp.concatenate(cols, axis=1), k_ref, vt_ref, acc_ref, i, tk, 2)

    lq = lam_ref[...]
    lam = (jnp.exp(jnp.sum(lq[0:1] * lq[1:2], axis=-1, keepdims=True))
           - jnp.exp(jnp.sum(lq[2:3] * lq[3:4], axis=-1, keepdims=True)) + LAMBDA_INIT)
    for s in range(subtiles):
        c = 2 * s * tk
        od = (ot[:, c:c + tk] - lam * ot[:, c + tk:c + 2 * tk]).T
        o_ref[s * tk:(s + 1) * tk, :] = (_rms_norm(od, g_ref[...]) * (1.0 - LAMBDA_INIT)).astype(BF16)


def _mla_attn_kernel(qt_ref, k_ref, vt_ref, o_ref, acc_ref):
    i = pl.program_id(2)
    subtiles, _, tk = qt_ref.shape
    qt = jnp.concatenate([qt_ref[s] for s in range(subtiles)], axis=1)
    ot = _causal_attention(qt, k_ref, vt_ref, acc_ref, i, tk, 1)
    for s in range(subtiles):
        o_ref[s * tk:(s + 1) * tk, :] = ot[:, s * tk:(s + 1) * tk].T.astype(BF16)


def _attn_call(body, name, qt, k, vt, extra, *, maps):
    B, H, nc, dk, tk = qt.shape
    S = k.shape[2]
    dv = vt.shape[3]
    subtiles = min(ATTN_COLUMNS // (maps * tk), nc)
    tq = subtiles * tk
    assert S % tq == 0
    return pl.pallas_call(
        body,
        grid=(B, H, S // tq),
        in_specs=[
            pl.BlockSpec((None, None, subtiles, dk, tk), lambda b, h, i: (b, h, i, 0, 0)),
            pl.BlockSpec((None, None, S, dk), lambda b, h, i: (b, h, 0, 0)),
            pl.BlockSpec((None, None, nc, dv, tk), lambda b, h, i: (b, h, 0, 0, 0)),
        ] + [pl.BlockSpec(e.shape, lambda b, h, i: (0, 0)) for e in extra],
        out_specs=pl.BlockSpec((None, None, tq, dv), lambda b, h, i: (b, h, i, 0)),
        out_shape=jax.ShapeDtypeStruct((B, H, S, dv), BF16),
        scratch_shapes=[pltpu.VMEM((dv, maps * tq), F32)],
        compiler_params=pltpu.CompilerParams(
            dimension_semantics=("parallel", "parallel", "arbitrary"), vmem_limit_bytes=VMEM_LIMIT_BYTES),
        name=name,
    )(qt, k, vt, *extra)


def _out_kernel(x_ref, lng_ref, lnb_ref, oa_ref, za_ref, ob_ref, zb_ref, g_ref, p_ref,
                woa_ref, wob_ref, wout_ref, wpg_ref, bpg_ref, wpp_ref, pog_ref, pob_ref, y_ref):
    xn = _layer_norm(x_ref[...], lng_ref[...], lnb_ref[...])
    oa = jnp.concatenate([oa_ref[h] for h in range(DIFF_HEADS)], axis=-1)
    ob = jnp.concatenate([ob_ref[h] for h in range(MLA_HEADS)], axis=-1)
    ya = _dot(oa * za_ref[...], woa_ref[...])
    yb = _dot(ob * zb_ref[...], wob_ref[...])
    g = g_ref[...]
    merged = g[:, :D_MODEL].astype(F32) * ya + g[:, D_MODEL:].astype(F32) * yb
    y = DEEPNORM_ALPHA * xn + _dot(merged.astype(BF16), wout_ref[...])
    gate = jax.nn.sigmoid(_dot(y.astype(BF16), wpg_ref[...]) + bpg_ref[...])
    y = y + gate * _dot(p_ref[...].astype(BF16), wpp_ref[...])
    y_ref[...] = _layer_norm(y, pog_ref[...], pob_ref[...])


def _out_call(x, lng, lnb, oa, za, ob, zb, g, p, woa, wob, wout, wpg, bpg, wpp, pog, pob, *, tm):
    B, S, D = x.shape
    row = lambda b, i: (b, i, 0)
    head = lambda b, i: (b, 0, i, 0)
    in_specs = [
        pl.BlockSpec((None, tm, D), row), _resident(lng.shape), _resident(lnb.shape),
        pl.BlockSpec((None, DIFF_HEADS, tm, HEAD_LANES), head),
        pl.BlockSpec((None, tm, DIFF_WIDTH), row),
        pl.BlockSpec((None, MLA_HEADS, tm, MLA_V), head),
        pl.BlockSpec((None, tm, MLA_WIDTH), row),
        pl.BlockSpec((None, tm, 2 * D), row),
        pl.BlockSpec((None, tm, PLE_DIM), row),
        _resident(woa.shape), _resident(wob.shape), _resident(wout.shape), _resident(wpg.shape),
        _resident(bpg.shape), _resident(wpp.shape), _resident(pog.shape), _resident(pob.shape),
    ]
    return pl.pallas_call(
        _out_kernel, grid=(B, S // tm), in_specs=in_specs,
        out_specs=pl.BlockSpec((None, tm, D), row),
        out_shape=jax.ShapeDtypeStruct((B, S, D), F32),
        compiler_params=pltpu.CompilerParams(
            dimension_semantics=("parallel", "parallel"), vmem_limit_bytes=VMEM_LIMIT_BYTES),
        name="merge_out",
    )(x, lng, lnb, oa, za, ob, zb, g, p, woa, wob, wout, wpg, bpg, wpp, pog, pob)


def kernel(x, p, positions, ln_emb_g, ln_emb_b, w_in, b_gate, diff_lambda, diff_subln_g, w_o_a,
           mla_q_norm_g, mla_w_uq, mla_kv_norm_g, mla_w_ukv, w_o_b, w_out, ple_w_gate, ple_b_gate,
           ple_w_proj, ln_post_g, ln_post_b):
    B, S, D = x.shape
    assert D == D_MODEL and w_in.shape[0] == DEPTH == 1
    tk = min(KEY_CHUNK, S)
    tm = tk
    assert S % tm == 0
    row2 = lambda v: v.reshape(1, -1)

    w = w_in[0].astype(BF16)
    ka0, va0, za0 = DIFF_QK_WIDTH, 2 * DIFF_QK_WIDTH, 2 * DIFF_QK_WIDTH + DIFF_WIDTH
    kpe_end = za0 + DIFF_WIDTH + MLA_Q_LORA + MLA_KV_LORA + MLA_ROPE
    win = jnp.concatenate([w[:, ka0:va0], w[:, za0:kpe_end],
                           jnp.zeros((D, LANES - MLA_ROPE), BF16), w[:, kpe_end:]], axis=1)
    wqat = w[:, :ka0].T
    wvat = w[:, va0:za0].T
    wuq = mla_w_uq[0].astype(BF16).reshape(MLA_Q_LORA, MLA_HEADS, MLA_NOPE + MLA_ROPE)
    wuqt = jnp.pad(wuq, ((0, 0), (0, 0), (0, MLA_QK_PAD - MLA_NOPE - MLA_ROPE))).reshape(MLA_Q_LORA, -1).T
    wukv = mla_w_ukv[0].astype(BF16).reshape(MLA_KV_LORA, MLA_HEADS, MLA_NOPE + MLA_V)
    wuk = wukv[:, :, :MLA_NOPE].reshape(MLA_KV_LORA, -1)
    wuvt = wukv[:, :, MLA_NOPE:].reshape(MLA_KV_LORA, -1).T
    inv = ROPE_THETA ** (-jnp.arange(0, DIFF_HEAD_DIM, 2, dtype=F32) / DIFF_HEAD_DIM)
    inv = jnp.tile(inv, LANES // inv.shape[0]).reshape(1, LANES)

    consts = (inv, row2(ln_emb_g), row2(ln_emb_b), win, wqat, wvat, row2(b_gate[0]),
              row2(mla_q_norm_g[0]), wuqt, row2(mla_kv_norm_g[0]), wuk, wuvt)
    qat, ka, vat, za, qbt, kb, vbt, zb, g = _proj_call(x, positions.reshape(B, S, 1), consts, tm=tm, tk=tk)
    oa = _attn_call(_diff_attn_kernel, "diff_attn", qat, ka, vat,
                    (diff_lambda[0], row2(diff_subln_g[0])), maps=2)
    ob = _attn_call(_mla_attn_kernel, "mla_attn", qbt, kb, vbt, (), maps=1)
    return _out_call(
        x, row2(ln_emb_g), row2(ln_emb_b), oa, za, ob, zb, g, p[0],
        w_o_a[0].astype(BF16), w_o_b[0].astype(BF16), w_out[0].astype(BF16), ple_w_gate[0].astype(BF16),
        row2(ple_b_gate[0]), ple_w_proj[0].astype(BF16), row2(ln_post_g[0]), row2(ln_post_b[0]), tm=tm)
```

```python
import functools
import math

import jax
import jax.numpy as jnp
from jax import lax
from jax.experimental import pallas as pl
from jax.experimental.pallas import tpu as pltpu

F32 = jnp.float32
BF16 = jnp.bfloat16

D_MODEL = 1024
DEPTH = 1
PLE_DIM = 256
ROPE_THETA = 10000.0
LN_EPS = 1e-5
RMS_EPS = 1e-6
NEG_INF = -1e30

DIFF_HEADS = 8
DIFF_HEAD_DIM = 64
DIFF_QK_WIDTH = 2 * DIFF_HEADS * DIFF_HEAD_DIM
DIFF_WIDTH = DIFF_HEADS * 2 * DIFF_HEAD_DIM

MLA_HEADS = 8
MLA_Q_LORA = 384
MLA_KV_LORA = 256
MLA_NOPE = 128
MLA_ROPE = 64
MLA_V = 128
MLA_WIDTH = MLA_HEADS * MLA_V

DEEPNORM_ALPHA = (2 * DEPTH) ** 0.25
LAMBDA_INIT = 0.8 - 0.6 * math.exp(-0.3 * 0)

LANES = 128
HEAD_LANES = 2 * DIFF_HEAD_DIM
MLA_QK_PAD = 256
LOG2E = math.log2(math.e)
HALF_ROT = DIFF_HEAD_DIM // 2

KEY_CHUNK = 256
ATTN_COLUMNS = 1024

OFF_KA = 0
OFF_ZA = OFF_KA + DIFF_QK_WIDTH
OFF_CQ = OFF_ZA + DIFF_WIDTH
OFF_CKV = OFF_CQ + MLA_Q_LORA
OFF_KPE = OFF_CKV + MLA_KV_LORA
OFF_ZB = OFF_KPE + LANES
OFF_G = OFF_ZB + MLA_WIDTH

VMEM_LIMIT_BYTES = 56 * 1024 * 1024


def _layer_norm(x, g, b):
    mu = jnp.mean(x, axis=-1, keepdims=True)
    xc = x - mu
    var = jnp.mean(xc * xc, axis=-1, keepdims=True)
    return xc * lax.rsqrt(var + LN_EPS) * g + b


def _rms_norm(x, g):
    return x * lax.rsqrt(jnp.mean(x * x, axis=-1, keepdims=True) + RMS_EPS) * g


def _dot(a, b):
    return jnp.dot(a, b, preferred_element_type=F32)


def _dot_nt(a, b):
    return lax.dot_general(a, b, (((1,), (1,)), ((), ())), preferred_element_type=F32)


def _proj_kernel(x_ref, pos_ref, inv_ref, lng_ref, lnb_ref, win_ref, wqat_ref, wvat_ref, bg_ref,
                 qng_ref, wuqt_ref, kvg_ref, wuk_ref, wuvt_ref,
                 qat_ref, ka_ref, vat_ref, za_ref, qbt_ref, kb_ref, vbt_ref, zb_ref, g_ref):
    tm = x_ref.shape[0]
    tk = qat_ref.shape[-1]
    xn = _layer_norm(x_ref[...], lng_ref[...], lnb_ref[...]).astype(BF16)

    ang = pos_ref[...].astype(F32) * inv_ref[...]
    cos = jnp.cos(ang)
    sin = jnp.sin(ang)
    lane = lax.broadcasted_iota(jnp.int32, (1, LANES), 1)
    first_half = (lane % DIFF_HEAD_DIM) < HALF_ROT
    sin_signed = jnp.where(first_half, -sin, sin)
    cos_t = cos.T
    sin_signed_t = sin_signed.T

    def rope(t):
        partner = jnp.where(first_half, pltpu.roll(t, LANES - HALF_ROT, 1), pltpu.roll(t, HALF_ROT, 1))
        return t * cos + partner * sin_signed

    def rope_t(t):
        h = HALF_ROT
        partner = jnp.concatenate([t[h:2 * h], t[0:h], t[3 * h:4 * h], t[2 * h:3 * h]], axis=0)
        return t * cos_t + partner * sin_signed_t

    def proj(off, width):
        return _dot(xn, win_ref[:, off:off + width])

    def store_chunks(ref, h, rows, t):
        for c in range(tm // tk):
            ref[h, c, rows, :] = t[:, c * tk:(c + 1) * tk].astype(BF16)

    all_rows = slice(None)
    qa_scale = DIFF_HEAD_DIM ** -0.5 * LOG2E
    hqt = _dot_nt(wqat_ref[...], xn)
    for h in range(DIFF_HEADS):
        store_chunks(qat_ref, h, all_rows, rope_t(hqt[h * HEAD_LANES:(h + 1) * HEAD_LANES]) * qa_scale)
    hk = proj(OFF_KA, DIFF_QK_WIDTH)
    for h in range(DIFF_HEADS):
        ka_ref[h] = rope(hk[:, h * HEAD_LANES:(h + 1) * HEAD_LANES]).astype(BF16)
    hvt = _dot_nt(wvat_ref[...], xn)
    for h in range(DIFF_HEADS):
        store_chunks(vat_ref, h, all_rows, hvt[h * HEAD_LANES:(h + 1) * HEAD_LANES])
    hz = proj(OFF_ZA, DIFF_WIDTH)
    za_ref[...] = (hz * jax.nn.sigmoid(hz)).astype(BF16)

    qb_scale = (MLA_NOPE + MLA_ROPE) ** -0.5 * LOG2E
    cq = _rms_norm(proj(OFF_CQ, MLA_Q_LORA), qng_ref[...]).astype(BF16)
    uqt = _dot_nt(wuqt_ref[...], cq)
    for h in range(MLA_HEADS):
        base = h * MLA_QK_PAD
        store_chunks(qbt_ref, h, slice(0, LANES), uqt[base:base + LANES] * qb_scale)
        store_chunks(qbt_ref, h, slice(LANES, 2 * LANES), rope_t(uqt[base + LANES:base + 2 * LANES]) * qb_scale)

    k_rot = rope(proj(OFF_KPE, LANES)).astype(BF16)
    ckv = _rms_norm(proj(OFF_CKV, MLA_KV_LORA), kvg_ref[...]).astype(BF16)
    uk = _dot(ckv, wuk_ref[...])
    for h in range(MLA_HEADS):
        kb_ref[h, :, 0:LANES] = uk[:, h * MLA_NOPE:(h + 1) * MLA_NOPE].astype(BF16)
        kb_ref[h, :, LANES:2 * LANES] = k_rot
    uvt = _dot_nt(wuvt_ref[...], ckv)
    for h in range(MLA_HEADS):
        store_chunks(vbt_ref, h, all_rows, uvt[h * MLA_V:(h + 1) * MLA_V])

    hzb = proj(OFF_ZB, MLA_WIDTH)
    zb_ref[...] = (hzb * jax.nn.sigmoid(hzb)).astype(BF16)
    g_ref[...] = jax.nn.sigmoid(proj(OFF_G, 2 * D_MODEL) + bg_ref[...]).astype(BF16)


def _resident(shape):
    nd = len(shape)
    return pl.BlockSpec(shape, lambda *_: (0,) * nd, pipeline_mode=pl.Buffered(1))


def _proj_call(x, pos, consts, *, tm, tk):
    B, S, D = x.shape
    grid = (B, S // tm)
    row = lambda b, i: (b, i, 0)
    head = lambda b, i: (b, 0, i, 0)
    head_t = lambda b, i: (b, 0, i, 0, 0)
    bf = lambda shape: jax.ShapeDtypeStruct(shape, BF16)
    nc = S // tk
    out_shape = [
        bf((B, DIFF_HEADS, nc, HEAD_LANES, tk)), bf((B, DIFF_HEADS, S, HEAD_LANES)),
        bf((B, DIFF_HEADS, nc, HEAD_LANES, tk)), bf((B, S, DIFF_WIDTH)),
        bf((B, MLA_HEADS, nc, MLA_QK_PAD, tk)), bf((B, MLA_HEADS, S, MLA_QK_PAD)),
        bf((B, MLA_HEADS, nc, MLA_V, tk)), bf((B, S, MLA_WIDTH)), bf((B, S, 2 * D)),
    ]
    out_specs = [
        pl.BlockSpec((None, DIFF_HEADS, tm // tk, HEAD_LANES, tk), head_t),
        pl.BlockSpec((None, DIFF_HEADS, tm, HEAD_LANES), head),
        pl.BlockSpec((None, DIFF_HEADS, tm // tk, HEAD_LANES, tk), head_t),
        pl.BlockSpec((None, tm, DIFF_WIDTH), row),
        pl.BlockSpec((None, MLA_HEADS, tm // tk, MLA_QK_PAD, tk), head_t),
        pl.BlockSpec((None, MLA_HEADS, tm, MLA_QK_PAD), head),
        pl.BlockSpec((None, MLA_HEADS, tm // tk, MLA_V, tk), head_t),
        pl.BlockSpec((None, tm, MLA_WIDTH), row),
        pl.BlockSpec((None, tm, 2 * D), row),
    ]
    in_specs = [pl.BlockSpec((None, tm, D), row), pl.BlockSpec((None, tm, 1), row)]
    in_specs += [_resident(c.shape) for c in consts]
    return pl.pallas_call(
        _proj_kernel, grid=grid, in_specs=in_specs, out_specs=out_specs, out_shape=out_shape,
        compiler_params=pltpu.CompilerParams(
            dimension_semantics=("parallel", "parallel"), vmem_limit_bytes=VMEM_LIMIT_BYTES),
        name="proj",
    )(x, pos, *consts)


class _AttnState:
    def __init__(self, qt_ref, s_ref, cmax_ref, m_ref, l_ref, acc_ref):
        self.qt = qt_ref
        self.s = s_ref
        self.cmax = cmax_ref
        self.m = m_ref
        self.l = l_ref
        self.acc = acc_ref

    @staticmethod
    def scratch_shapes(dk, dv, tk, nq):
        return [pltpu.VMEM((dk, nq), BF16), pltpu.VMEM((2, tk, nq), F32), pltpu.VMEM((2, 1, nq), F32),
                pltpu.VMEM((1, nq), F32), pltpu.VMEM((1, nq), F32), pltpu.VMEM((dv, nq), F32)]


def _causal_attention(st, k_ref, vt_ref, i, tk, maps):
    nq = st.qt.shape[1]
    group = maps * tk
    subtiles = nq // group
    assert subtiles % 2 == 0 or subtiles == 1
    n = i * subtiles

    st.acc[...] = jnp.zeros_like(st.acc)
    st.m[...] = jnp.full_like(st.m, NEG_INF)
    st.l[...] = jnp.zeros_like(st.l)

    def keys(c):
        return k_ref[pl.ds(pl.multiple_of(c * tk, tk), tk), :]

    def diag_keep(r):
        c0 = r * group
        col = c0 + lax.broadcasted_iota(jnp.int32, (tk, nq - c0), 1)
        query = (col // group) * tk + col % tk
        key = r * tk + lax.broadcasted_iota(jnp.int32, (tk, nq - c0), 0)
        return key <= query

    def scores(c, slot, keep):
        s = _dot(keys(c), st.qt[...])
        if keep is not None:
            s = jnp.where(keep, s, NEG_INF)
        st.s[slot] = s
        st.cmax[slot] = jnp.max(s, axis=0, keepdims=True)

    def softmax_pv(c, slot):
        m_prev = st.m[...]
        m_new = jnp.maximum(m_prev, st.cmax[slot])
        pt = jnp.exp2(st.s[slot] - m_new)
        alpha = jnp.exp2(m_prev - m_new)
        st.l[...] = alpha * st.l[...] + jnp.sum(pt, axis=0, keepdims=True)
        st.m[...] = m_new
        st.acc[...] = alpha * st.acc[...] + _dot(vt_ref[c], pt.astype(BF16))

    keep0 = diag_keep(0)
    scores(0, 0, jnp.logical_or(keep0, n > 0))

    def pair(t, carry):
        c = 2 * t
        scores(c + 1, 1, None)
        softmax_pv(c, 0)
        scores(c + 2, 0, None)
        softmax_pv(c + 1, 1)
        return carry

    lax.fori_loop(0, jnp.maximum(n // 2 - 1, 0), pair, 0)

    @pl.when(n > 0)
    def _():
        scores(n - 1, 1, None)
        softmax_pv(n - 2, 0)
        scores(n, 0, keep0)
        softmax_pv(n - 1, 1)

    softmax_pv(n, 0)

    for r in range(1, subtiles):
        c0 = r * group
        s = jnp.where(diag_keep(r), _dot(keys(n + r), st.qt[:, c0:]), NEG_INF)
        m_prev = st.m[:, c0:]
        m_new = jnp.maximum(m_prev, jnp.max(s, axis=0, keepdims=True))
        pt = jnp.exp2(s - m_new)
        alpha = jnp.exp2(m_prev - m_new)
        st.l[:, c0:] = alpha * st.l[:, c0:] + jnp.sum(pt, axis=0, keepdims=True)
        st.m[:, c0:] = m_new
        st.acc[:, c0:] = alpha * st.acc[:, c0:] + _dot(vt_ref[n + r], pt.astype(BF16))
    return st.acc[...] / st.l[...]


def _diff_attn_kernel(qt_ref, k_ref, vt_ref, lam_ref, g_ref, o_ref, *scratch):
    i = pl.program_id(2)
    subtiles, dk, tk = qt_ref.shape
    st = _AttnState(*scratch)
    feature = lax.broadcasted_iota(jnp.int32, (dk, 1), 0)
    for s in range(subtiles):
        q = qt_ref[s]
        zero = jnp.zeros_like(q)
        st.qt[:, 2 * s * tk:(2 * s + 1) * tk] = jnp.where(feature < DIFF_HEAD_DIM, q, zero)
        st.qt[:, (2 * s + 1) * tk:(2 * s + 2) * tk] = jnp.where(feature >= DIFF_HEAD_DIM, q, zero)
    ot = _causal_attention(st, k_ref, vt_ref, i, tk, 2)

    lq = lam_ref[...]
    lam = (jnp.exp(jnp.sum(lq[0:1] * lq[1:2], axis=-1, keepdims=True))
           - jnp.exp(jnp.sum(lq[2:3] * lq[3:4], axis=-1, keepdims=True)) + LAMBDA_INIT)
    for s in range(subtiles):
        c = 2 * s * tk
        od = (ot[:, c:c + tk] - lam * ot[:, c + tk:c + 2 * tk]).T
        o_ref[s * tk:(s + 1) * tk, :] = (_rms_norm(od, g_ref[...]) * (1.0 - LAMBDA_INIT)).astype(BF16)


def _mla_attn_kernel(qt_ref, k_ref, vt_ref, o_ref, *scratch):
    i = pl.program_id(2)
    subtiles, _, tk = qt_ref.shape
    st = _AttnState(*scratch)
    for s in range(subtiles):
        st.qt[:, s * tk:(s + 1) * tk] = qt_ref[s]
    ot = _causal_attention(st, k_ref, vt_ref, i, tk, 1)
    for s in range(subtiles):
        o_ref[s * tk:(s + 1) * tk, :] = ot[:, s * tk:(s + 1) * tk].T.astype(BF16)


def _attn_call(body, name, qt, k, vt, extra, *, maps):
    B, H, nc, dk, tk = qt.shape
    S = k.shape[2]
    dv = vt.shape[3]
    subtiles = min(ATTN_COLUMNS // (maps * tk), nc)
    tq = subtiles * tk
    assert S % tq == 0
    return pl.pallas_call(
        body,
        grid=(B, H, S // tq),
        in_specs=[
            pl.BlockSpec((None, None, subtiles, dk, tk), lambda b, h, i: (b, h, i, 0, 0)),
            pl.BlockSpec((None, None, S, dk), lambda b, h, i: (b, h, 0, 0)),
            pl.BlockSpec((None, None, nc, dv, tk), lambda b, h, i: (b, h, 0, 0, 0)),
        ] + [pl.BlockSpec(e.shape, lambda b, h, i: (0, 0)) for e in extra],
        out_specs=pl.BlockSpec((None, None, tq, dv), lambda b, h, i: (b, h, i, 0)),
        out_shape=jax.ShapeDtypeStruct((B, H, S, dv), BF16),
        scratch_shapes=_AttnState.scratch_shapes(dk, dv, tk, maps * tq),
        compiler_params=pltpu.CompilerParams(
            dimension_semantics=("parallel", "parallel", "arbitrary"), vmem_limit_bytes=VMEM_LIMIT_BYTES),
        name=name,
    )(qt, k, vt, *extra)


def _out_kernel(x_ref, lng_ref, lnb_ref, oa_ref, za_ref, ob_ref, zb_ref, g_ref, p_ref,
                woa_ref, wob_ref, wout_ref, wpg_ref, bpg_ref, wpp_ref, pog_ref, pob_ref, y_ref):
    xn = _layer_norm(x_ref[...], lng_ref[...], lnb_ref[...])
    oa = jnp.concatenate([oa_ref[h] for h in range(DIFF_HEADS)], axis=-1)
    ob = jnp.concatenate([ob_ref[h] for h in range(MLA_HEADS)], axis=-1)
    ya = _dot(oa * za_ref[...], woa_ref[...])
    yb = _dot(ob * zb_ref[...], wob_ref[...])
    g = g_ref[...]
    merged = g[:, :D_MODEL].astype(F32) * ya + g[:, D_MODEL:].astype(F32) * yb
    y = DEEPNORM_ALPHA * xn + _dot(merged.astype(BF16), wout_ref[...])
    gate = jax.nn.sigmoid(_dot(y.astype(BF16), wpg_ref[...]) + bpg_ref[...])
    y = y + gate * _dot(p_ref[...].astype(BF16), wpp_ref[...])
    y_ref[...] = _layer_norm(y, pog_ref[...], pob_ref[...])


def _out_call(x, lng, lnb, oa, za, ob, zb, g, p, woa, wob, wout, wpg, bpg, wpp, pog, pob, *, tm):
    B, S, D = x.shape
    row = lambda b, i: (b, i, 0)
    head = lambda b, i: (b, 0, i, 0)
    in_specs = [
        pl.BlockSpec((None, tm, D), row), _resident(lng.shape), _resident(lnb.shape),
        pl.BlockSpec((None, DIFF_HEADS, tm, HEAD_LANES), head),
        pl.BlockSpec((None, tm, DIFF_WIDTH), row),
        pl.BlockSpec((None, MLA_HEADS, tm, MLA_V), head),
        pl.BlockSpec((None, tm, MLA_WIDTH), row),
        pl.BlockSpec((None, tm, 2 * D), row),
        pl.BlockSpec((None, tm, PLE_DIM), row),
        _resident(woa.shape), _resident(wob.shape), _resident(wout.shape), _resident(wpg.shape),
        _resident(bpg.shape), _resident(wpp.shape), _resident(pog.shape), _resident(pob.shape),
    ]
    return pl.pallas_call(
        _out_kernel, grid=(B, S // tm), in_specs=in_specs,
        out_specs=pl.BlockSpec((None, tm, D), row),
        out_shape=jax.ShapeDtypeStruct((B, S, D), F32),
        compiler_params=pltpu.CompilerParams(
            dimension_semantics=("parallel", "parallel"), vmem_limit_bytes=VMEM_LIMIT_BYTES),
        name="merge_out",
    )(x, lng, lnb, oa, za, ob, zb, g, p, woa, wob, wout, wpg, bpg, wpp, pog, pob)


def kernel(x, p, positions, ln_emb_g, ln_emb_b, w_in, b_gate, diff_lambda, diff_subln_g, w_o_a,
           mla_q_norm_g, mla_w_uq, mla_kv_norm_g, mla_w_ukv, w_o_b, w_out, ple_w_gate, ple_b_gate,
           ple_w_proj, ln_post_g, ln_post_b):
    B, S, D = x.shape
    assert D == D_MODEL and w_in.shape[0] == DEPTH == 1
    tk = min(KEY_CHUNK, S)
    tm = tk
    assert S % tm == 0
    row2 = lambda v: v.reshape(1, -1)

    w = w_in[0].astype(BF16)
    ka0, va0, za0 = DIFF_QK_WIDTH, 2 * DIFF_QK_WIDTH, 2 * DIFF_QK_WIDTH + DIFF_WIDTH
    kpe_end = za0 + DIFF_WIDTH + MLA_Q_LORA + MLA_KV_LORA + MLA_ROPE
    win = jnp.concatenate([w[:, ka0:va0], w[:, za0:kpe_end],
                           jnp.zeros((D, LANES - MLA_ROPE), BF16), w[:, kpe_end:]], axis=1)
    wqat = w[:, :ka0].T
    wvat = w[:, va0:za0].T
    wuq = mla_w_uq[0].astype(BF16).reshape(MLA_Q_LORA, MLA_HEADS, MLA_NOPE + MLA_ROPE)
    wuqt = jnp.pad(wuq, ((0, 0), (0, 0), (0, MLA_QK_PAD - MLA_NOPE - MLA_ROPE))).reshape(MLA_Q_LORA, -1).T
    wukv = mla_w_ukv[0].astype(BF16).reshape(MLA_KV_LORA, MLA_HEADS, MLA_NOPE + MLA_V)
    wuk = wukv[:, :, :MLA_NOPE].reshape(MLA_KV_LORA, -1)
    wuvt = wukv[:, :, MLA_NOPE:].reshape(MLA_KV_LORA, -1).T
    inv = ROPE_THETA ** (-jnp.arange(0, DIFF_HEAD_DIM, 2, dtype=F32) / DIFF_HEAD_DIM)
    inv = jnp.tile(inv, LANES // inv.shape[0]).reshape(1, LANES)

    consts = (inv, row2(ln_emb_g), row2(ln_emb_b), win, wqat, wvat, row2(b_gate[0]),
              row2(mla_q_norm_g[0]), wuqt, row2(mla_kv_norm_g[0]), wuk, wuvt)
    qat, ka, vat, za, qbt, kb, vbt, zb, g = _proj_call(x, positions.reshape(B, S, 1), consts, tm=tm, tk=tk)
    oa = _attn_call(_diff_attn_kernel, "diff_attn", qat, ka, vat,
                    (diff_lambda[0], row2(diff_subln_g[0])), maps=2)
    ob = _attn_call(_mla_attn_kernel, "mla_attn", qbt, kb, vbt, (), maps=1)
    return _out_call(
        x, row2(ln_emb_g), row2(ln_emb_b), oa, za, ob, zb, g, p[0],
        w_o_a[0].astype(BF16), w_o_b[0].astype(BF16), w_out[0].astype(BF16), ple_w_gate[0].astype(BF16),
        row2(ple_b_gate[0]), ple_w_proj[0].astype(BF16), row2(ln_post_g[0]), row2(ln_post_b[0]), tm=tm)
```

```python
import functools
import math

import jax
import jax.numpy as jnp
from jax import lax
from jax.experimental import pallas as pl
from jax.experimental.pallas import tpu as pltpu

F32 = jnp.float32
BF16 = jnp.bfloat16

D_MODEL = 1024
DEPTH = 1
PLE_DIM = 256
ROPE_THETA = 10000.0
LN_EPS = 1e-5
RMS_EPS = 1e-6
NEG_INF = -1e30

DIFF_HEADS = 8
DIFF_HEAD_DIM = 64
DIFF_QK_WIDTH = 2 * DIFF_HEADS * DIFF_HEAD_DIM
DIFF_WIDTH = DIFF_HEADS * 2 * DIFF_HEAD_DIM

MLA_HEADS = 8
MLA_Q_LORA = 384
MLA_KV_LORA = 256
MLA_NOPE = 128
MLA_ROPE = 64
MLA_V = 128
MLA_WIDTH = MLA_HEADS * MLA_V

DEEPNORM_ALPHA = (2 * DEPTH) ** 0.25
LAMBDA_INIT = 0.8 - 0.6 * math.exp(-0.3 * 0)

LANES = 128
HEAD_LANES = 2 * DIFF_HEAD_DIM
MLA_QK_PAD = 256
LOG2E = math.log2(math.e)
HALF_ROT = DIFF_HEAD_DIM // 2

KEY_CHUNK = 256
ATTN_COLUMNS = 1024

OFF_KA = 0
OFF_ZA = OFF_KA + DIFF_QK_WIDTH
OFF_CQ = OFF_ZA + DIFF_WIDTH
OFF_CKV = OFF_CQ + MLA_Q_LORA
OFF_KPE = OFF_CKV + MLA_KV_LORA
OFF_ZB = OFF_KPE + LANES
OFF_G = OFF_ZB + MLA_WIDTH

VMEM_LIMIT_BYTES = 56 * 1024 * 1024


def _layer_norm(x, g, b):
    mu = jnp.mean(x, axis=-1, keepdims=True)
    xc = x - mu
    var = jnp.mean(xc * xc, axis=-1, keepdims=True)
    return xc * lax.rsqrt(var + LN_EPS) * g + b


def _rms_norm(x, g):
    return x * lax.rsqrt(jnp.mean(x * x, axis=-1, keepdims=True) + RMS_EPS) * g


def _dot(a, b):
    return jnp.dot(a, b, preferred_element_type=F32)


def _dot_nt(a, b):
    return lax.dot_general(a, b, (((1,), (1,)), ((), ())), preferred_element_type=F32)


def _proj_kernel(x_ref, pos_ref, inv_ref, lng_ref, lnb_ref, win_ref, wqat_ref, wvat_ref, bg_ref,
                 qng_ref, wuqt_ref, kvg_ref, wuk_ref, wuvt_ref,
                 qat_ref, ka_ref, vat_ref, za_ref, qbt_ref, kb_ref, vbt_ref, zb_ref, g_ref):
    tm = x_ref.shape[0]
    tk = qat_ref.shape[-1]
    xn = _layer_norm(x_ref[...], lng_ref[...], lnb_ref[...]).astype(BF16)

    ang = pos_ref[...].astype(F32) * inv_ref[...]
    cos = jnp.cos(ang)
    sin = jnp.sin(ang)
    lane = lax.broadcasted_iota(jnp.int32, (1, LANES), 1)
    first_half = (lane % DIFF_HEAD_DIM) < HALF_ROT
    sin_signed = jnp.where(first_half, -sin, sin)
    cos_t = cos.T
    sin_signed_t = sin_signed.T

    def rope(t):
        partner = jnp.where(first_half, pltpu.roll(t, LANES - HALF_ROT, 1), pltpu.roll(t, HALF_ROT, 1))
        return t * cos + partner * sin_signed

    def rope_t(t):
        h = HALF_ROT
        partner = jnp.concatenate([t[h:2 * h], t[0:h], t[3 * h:4 * h], t[2 * h:3 * h]], axis=0)
        return t * cos_t + partner * sin_signed_t

    def proj(off, width):
        return _dot(xn, win_ref[:, off:off + width])

    def store_chunks(ref, h, rows, t):
        for c in range(tm // tk):
            ref[h, c, rows, :] = t[:, c * tk:(c + 1) * tk].astype(BF16)

    all_rows = slice(None)
    qa_scale = DIFF_HEAD_DIM ** -0.5 * LOG2E
    hqt = _dot_nt(wqat_ref[...], xn)
    for h in range(DIFF_HEADS):
        store_chunks(qat_ref, h, all_rows, rope_t(hqt[h * HEAD_LANES:(h + 1) * HEAD_LANES]) * qa_scale)
    hk = proj(OFF_KA, DIFF_QK_WIDTH)
    for h in range(DIFF_HEADS):
        ka_ref[h] = rope(hk[:, h * HEAD_LANES:(h + 1) * HEAD_LANES]).astype(BF16)
    hvt = _dot_nt(wvat_ref[...], xn)
    for h in range(DIFF_HEADS):
        store_chunks(vat_ref, h, all_rows, hvt[h * HEAD_LANES:(h + 1) * HEAD_LANES])
    hz = proj(OFF_ZA, DIFF_WIDTH)
    za_ref[...] = (hz * jax.nn.sigmoid(hz)).astype(BF16)

    qb_scale = (MLA_NOPE + MLA_ROPE) ** -0.5 * LOG2E
    cq = _rms_norm(proj(OFF_CQ, MLA_Q_LORA), qng_ref[...]).astype(BF16)
    uqt = _dot_nt(wuqt_ref[...], cq)
    for h in range(MLA_HEADS):
        base = h * MLA_QK_PAD
        store_chunks(qbt_ref, h, slice(0, LANES), uqt[base:base + LANES] * qb_scale)
        store_chunks(qbt_ref, h, slice(LANES, 2 * LANES), rope_t(uqt[base + LANES:base + 2 * LANES]) * qb_scale)

    k_rot = rope(proj(OFF_KPE, LANES)).astype(BF16)
    ckv = _rms_norm(proj(OFF_CKV, MLA_KV_LORA), kvg_ref[...]).astype(BF16)
    uk = _dot(ckv, wuk_ref[...])
    for h in range(MLA_HEADS):
        kb_ref[h, :, 0:LANES] = uk[:, h * MLA_NOPE:(h + 1) * MLA_NOPE].astype(BF16)
        kb_ref[h, :, LANES:2 * LANES] = k_rot
    uvt = _dot_nt(wuvt_ref[...], ckv)
    for h in range(MLA_HEADS):
        store_chunks(vbt_ref, h, all_rows, uvt[h * MLA_V:(h + 1) * MLA_V])

    hzb = proj(OFF_ZB, MLA_WIDTH)
    zb_ref[...] = (hzb * jax.nn.sigmoid(hzb)).astype(BF16)
    g_ref[...] = jax.nn.sigmoid(proj(OFF_G, 2 * D_MODEL) + bg_ref[...]).astype(BF16)


def _resident(shape):
    nd = len(shape)
    return pl.BlockSpec(shape, lambda *_: (0,) * nd, pipeline_mode=pl.Buffered(1))


def _proj_call(x, pos, consts, *, tm, tk):
    B, S, D = x.shape
    grid = (B, S // tm)
    row = lambda b, i: (b, i, 0)
    head = lambda b, i: (b, 0, i, 0)
    head_t = lambda b, i: (b, 0, i, 0, 0)
    bf = lambda shape: jax.ShapeDtypeStruct(shape, BF16)
    nc = S // tk
    out_shape = [
        bf((B, DIFF_HEADS, nc, HEAD_LANES, tk)), bf((B, DIFF_HEADS, S, HEAD_LANES)),
        bf((B, DIFF_HEADS, nc, HEAD_LANES, tk)), bf((B, S, DIFF_WIDTH)),
        bf((B, MLA_HEADS, nc, MLA_QK_PAD, tk)), bf((B, MLA_HEADS, S, MLA_QK_PAD)),
        bf((B, MLA_HEADS, nc, MLA_V, tk)), bf((B, S, MLA_WIDTH)), bf((B, S, 2 * D)),
    ]
    out_specs = [
        pl.BlockSpec((None, DIFF_HEADS, tm // tk, HEAD_LANES, tk), head_t),
        pl.BlockSpec((None, DIFF_HEADS, tm, HEAD_LANES), head),
        pl.BlockSpec((None, DIFF_HEADS, tm // tk, HEAD_LANES, tk), head_t),
        pl.BlockSpec((None, tm, DIFF_WIDTH), row),
        pl.BlockSpec((None, MLA_HEADS, tm // tk, MLA_QK_PAD, tk), head_t),
        pl.BlockSpec((None, MLA_HEADS, tm, MLA_QK_PAD), head),
        pl.BlockSpec((None, MLA_HEADS, tm // tk, MLA_V, tk), head_t),
        pl.BlockSpec((None, tm, MLA_WIDTH), row),
        pl.BlockSpec((None, tm, 2 * D), row),
    ]
    in_specs = [pl.BlockSpec((None, tm, D), row), pl.BlockSpec((None, tm, 1), row)]
    in_specs += [_resident(c.shape) for c in consts]
    return pl.pallas_call(
        _proj_kernel, grid=grid, in_specs=in_specs, out_specs=out_specs, out_shape=out_shape,
        compiler_params=pltpu.CompilerParams(
            dimension_semantics=("parallel", "parallel"), vmem_limit_bytes=VMEM_LIMIT_BYTES),
        name="proj",
    )(x, pos, *consts)


SUM_ROWS = 16


class _AttnState:
    def __init__(self, qt_ref, s_ref, cmax_ref, m_ref, acc_ref):
        self.qt = qt_ref
        self.s = s_ref
        self.cmax = cmax_ref
        self.m = m_ref
        self.acc = acc_ref

    @staticmethod
    def scratch_shapes(dk, dv, tk, nq):
        return [pltpu.VMEM((dk, nq), BF16), pltpu.VMEM((2, tk, nq), F32), pltpu.VMEM((2, 1, nq), F32),
                pltpu.VMEM((1, nq), F32), pltpu.VMEM((dv + SUM_ROWS, nq), F32)]


def _causal_attention(st, k_ref, vt_ref, i, tk, maps):
    nq = st.qt.shape[1]
    dv = vt_ref.shape[1]
    group = maps * tk
    subtiles = nq // group
    assert subtiles % 2 == 0 or subtiles == 1
    n = i * subtiles
    ones = jnp.ones((SUM_ROWS, tk), BF16)

    st.acc[...] = jnp.zeros_like(st.acc)
    st.m[...] = jnp.full_like(st.m, NEG_INF)

    def diag_keep(r):
        c0 = r * group
        col = c0 + lax.broadcasted_iota(jnp.int32, (tk, nq - c0), 1)
        query = (col // group) * tk + col % tk
        key = r * tk + lax.broadcasted_iota(jnp.int32, (tk, nq - c0), 0)
        return key <= query

    def scores(c, slot, keep, c0=0):
        k = k_ref[pl.ds(pl.multiple_of(c * tk, tk), tk), :]
        s = _dot(k, st.qt[:, c0:])
        if keep is not None:
            s = jnp.where(keep, s, NEG_INF)
        st.s[slot, :, c0:] = s
        st.cmax[slot, :, c0:] = jnp.max(s, axis=0, keepdims=True)

    def softmax_pv(c, slot, c0=0):
        m_prev = st.m[:, c0:]
        m_new = jnp.maximum(m_prev, st.cmax[slot, :, c0:])
        pt = jnp.exp2(st.s[slot, :, c0:] - m_new).astype(BF16)
        st.m[:, c0:] = m_new
        values = jnp.concatenate([vt_ref[c], ones], axis=0)
        st.acc[:, c0:] = jnp.exp2(m_prev - m_new) * st.acc[:, c0:] + _dot(values, pt)

    scores(0, 0, None)

    def pair(t, carry):
        c = 2 * t
        scores(c + 1, 1, None)
        softmax_pv(c, 0)
        scores(c + 2, 0, None)
        softmax_pv(c + 1, 1)
        return carry

    lax.fori_loop(0, n // 2, pair, 0)

    s0 = jnp.where(diag_keep(0), st.s[0], NEG_INF)
    st.s[0] = s0
    st.cmax[0] = jnp.max(s0, axis=0, keepdims=True)
    for r in range(1, subtiles):
        scores(n + r, r % 2, diag_keep(r), r * group)
        softmax_pv(n + r - 1, (r - 1) % 2, (r - 1) * group)
    softmax_pv(n + subtiles - 1, (subtiles - 1) % 2, (subtiles - 1) * group)
    return st.acc[0:dv, :] / st.acc[dv:dv + 1, :]


def _diff_attn_kernel(qt_ref, k_ref, vt_ref, lam_ref, g_ref, o_ref, *scratch):
    i = pl.program_id(2)
    subtiles, dk, tk = qt_ref.shape
    st = _AttnState(*scratch)
    feature = lax.broadcasted_iota(jnp.int32, (dk, 1), 0)
    for s in range(subtiles):
        q = qt_ref[s]
        zero = jnp.zeros_like(q)
        st.qt[:, 2 * s * tk:(2 * s + 1) * tk] = jnp.where(feature < DIFF_HEAD_DIM, q, zero)
        st.qt[:, (2 * s + 1) * tk:(2 * s + 2) * tk] = jnp.where(feature >= DIFF_HEAD_DIM, q, zero)
    ot = _causal_attention(st, k_ref, vt_ref, i, tk, 2)

    lq = lam_ref[...]
    lam = (jnp.exp(jnp.sum(lq[0:1] * lq[1:2], axis=-1, keepdims=True))
           - jnp.exp(jnp.sum(lq[2:3] * lq[3:4], axis=-1, keepdims=True)) + LAMBDA_INIT)
    for s in range(subtiles):
        c = 2 * s * tk
        od = (ot[:, c:c + tk] - lam * ot[:, c + tk:c + 2 * tk]).T
        o_ref[s * tk:(s + 1) * tk, :] = (_rms_norm(od, g_ref[...]) * (1.0 - LAMBDA_INIT)).astype(BF16)


def _mla_attn_kernel(qt_ref, k_ref, vt_ref, o_ref, *scratch):
    i = pl.program_id(2)
    subtiles, _, tk = qt_ref.shape
    st = _AttnState(*scratch)
    for s in range(subtiles):
        st.qt[:, s * tk:(s + 1) * tk] = qt_ref[s]
    ot = _causal_attention(st, k_ref, vt_ref, i, tk, 1)
    for s in range(subtiles):
        o_ref[s * tk:(s + 1) * tk, :] = ot[:, s * tk:(s + 1) * tk].T.astype(BF16)


def _attn_call(body, name, qt, k, vt, extra, *, maps):
    B, H, nc, dk, tk = qt.shape
    S = k.shape[2]
    dv = vt.shape[3]
    subtiles = min(ATTN_COLUMNS // (maps * tk), nc)
    tq = subtiles * tk
    assert S % tq == 0
    return pl.pallas_call(
        body,
        grid=(B, H, S // tq),
        in_specs=[
            pl.BlockSpec((None, None, subtiles, dk, tk), lambda b, h, i: (b, h, i, 0, 0)),
            pl.BlockSpec((None, None, S, dk), lambda b, h, i: (b, h, 0, 0)),
            pl.BlockSpec((None, None, nc, dv, tk), lambda b, h, i: (b, h, 0, 0, 0)),
        ] + [pl.BlockSpec(e.shape, lambda b, h, i: (0, 0)) for e in extra],
        out_specs=pl.BlockSpec((None, None, tq, dv), lambda b, h, i: (b, h, i, 0)),
        out_shape=jax.ShapeDtypeStruct((B, H, S, dv), BF16),
        scratch_shapes=_AttnState.scratch_shapes(dk, dv, tk, maps * tq),
        compiler_params=pltpu.CompilerParams(
            dimension_semantics=("parallel", "parallel", "arbitrary"), vmem_limit_bytes=VMEM_LIMIT_BYTES),
        name=name,
    )(qt, k, vt, *extra)


def _out_kernel(x_ref, lng_ref, lnb_ref, oa_ref, za_ref, ob_ref, zb_ref, g_ref, p_ref,
                woa_ref, wob_ref, wout_ref, wpg_ref, bpg_ref, wpp_ref, pog_ref, pob_ref, y_ref):
    xn = _layer_norm(x_ref[...], lng_ref[...], lnb_ref[...])
    oa = jnp.concatenate([oa_ref[h] for h in range(DIFF_HEADS)], axis=-1)
    ob = jnp.concatenate([ob_ref[h] for h in range(MLA_HEADS)], axis=-1)
    ya = _dot(oa * za_ref[...], woa_ref[...])
    yb = _dot(ob * zb_ref[...], wob_ref[...])
    g = g_ref[...]
    merged = g[:, :D_MODEL].astype(F32) * ya + g[:, D_MODEL:].astype(F32) * yb
    y = DEEPNORM_ALPHA * xn + _dot(merged.astype(BF16), wout_ref[...])
    gate = jax.nn.sigmoid(_dot(y.astype(BF16), wpg_ref[...]) + bpg_ref[...])
    y = y + gate * _dot(p_ref[...].astype(BF16), wpp_ref[...])
    y_ref[...] = _layer_norm(y, pog_ref[...], pob_ref[...])


def _out_call(x, lng, lnb, oa, za, ob, zb, g, p, woa, wob, wout, wpg, bpg, wpp, pog, pob, *, tm):
    B, S, D = x.shape
    row = lambda b, i: (b, i, 0)
    head = lambda b, i: (b, 0, i, 0)
    in_specs = [
        pl.BlockSpec((None, tm, D), row), _resident(lng.shape), _resident(lnb.shape),
        pl.BlockSpec((None, DIFF_HEADS, tm, HEAD_LANES), head),
        pl.BlockSpec((None, tm, DIFF_WIDTH), row),
        pl.BlockSpec((None, MLA_HEADS, tm, MLA_V), head),
        pl.BlockSpec((None, tm, MLA_WIDTH), row),
        pl.BlockSpec((None, tm, 2 * D), row),
        pl.BlockSpec((None, tm, PLE_DIM), row),
        _resident(woa.shape), _resident(wob.shape), _resident(wout.shape), _resident(wpg.shape),
        _resident(bpg.shape), _resident(wpp.shape), _resident(pog.shape), _resident(pob.shape),
    ]
    return pl.pallas_call(
        _out_kernel, grid=(B, S // tm), in_specs=in_specs,
        out_specs=pl.BlockSpec((None, tm, D), row),
        out_shape=jax.ShapeDtypeStruct((B, S, D), F32),
        compiler_params=pltpu.CompilerParams(
            dimension_semantics=("parallel", "parallel"), vmem_limit_bytes=VMEM_LIMIT_BYTES),
        name="merge_out",
    )(x, lng, lnb, oa, za, ob, zb, g, p, woa, wob, wout, wpg, bpg, wpp, pog, pob)


def kernel(x, p, positions, ln_emb_g, ln_emb_b, w_in, b_gate, diff_lambda, diff_subln_g, w_o_a,
           mla_q_norm_g, mla_w_uq, mla_kv_norm_g, mla_w_ukv, w_o_b, w_out, ple_w_gate, ple_b_gate,
           ple_w_proj, ln_post_g, ln_post_b):
    B, S, D = x.shape
    assert D == D_MODEL and w_in.shape[0] == DEPTH == 1
    tk = min(KEY_CHUNK, S)
    tm = tk
    assert S % tm == 0
    row2 = lambda v: v.reshape(1, -1)

    w = w_in[0].astype(BF16)
    ka0, va0, za0 = DIFF_QK_WIDTH, 2 * DIFF_QK_WIDTH, 2 * DIFF_QK_WIDTH + DIFF_WIDTH
    kpe_end = za0 + DIFF_WIDTH + MLA_Q_LORA + MLA_KV_LORA + MLA_ROPE
    win = jnp.concatenate([w[:, ka0:va0], w[:, za0:kpe_end],
                           jnp.zeros((D, LANES - MLA_ROPE), BF16), w[:, kpe_end:]], axis=1)
    wqat = w[:, :ka0].T
    wvat = w[:, va0:za0].T
    wuq = mla_w_uq[0].astype(BF16).reshape(MLA_Q_LORA, MLA_HEADS, MLA_NOPE + MLA_ROPE)
    wuqt = jnp.pad(wuq, ((0, 0), (0, 0), (0, MLA_QK_PAD - MLA_NOPE - MLA_ROPE))).reshape(MLA_Q_LORA, -1).T
    wukv = mla_w_ukv[0].astype(BF16).reshape(MLA_KV_LORA, MLA_HEADS, MLA_NOPE + MLA_V)
    wuk = wukv[:, :, :MLA_NOPE].reshape(MLA_KV_LORA, -1)
    wuvt = wukv[:, :, MLA_NOPE:].reshape(MLA_KV_LORA, -1).T
    inv = ROPE_THETA ** (-jnp.arange(0, DIFF_HEAD_DIM, 2, dtype=F32) / DIFF_HEAD_DIM)
    inv = jnp.tile(inv, LANES // inv.shape[0]).reshape(1, LANES)

    consts = (inv, row2(ln_emb_g), row2(ln_emb_b), win, wqat, wvat, row2(b_gate[0]),
              row2(mla_q_norm_g[0]), wuqt, row2(mla_kv_norm_g[0]), wuk, wuvt)
    qat, ka, vat, za, qbt, kb, vbt, zb, g = _proj_call(x, positions.reshape(B, S, 1), consts, tm=tm, tk=tk)
    oa = _attn_call(_diff_attn_kernel, "diff_attn", qat, ka, vat,
                    (diff_lambda[0], row2(diff_subln_g[0])), maps=2)
    ob = _attn_call(_mla_attn_kernel, "mla_attn", qbt, kb, vbt, (), maps=1)
    return _out_call(
        x, row2(ln_emb_g), row2(ln_emb_b), oa, za, ob, zb, g, p[0],
        w_o_a[0].astype(BF16), w_o_b[0].astype(BF16), w_out[0].astype(BF16), ple_w_gate[0].astype(BF16),
        row2(ple_b_gate[0]), ple_w_proj[0].astype(BF16), row2(ln_post_g[0]), row2(ln_post_b[0]), tm=tm)
```

```python
import functools
import math

import jax
import jax.numpy as jnp
from jax import lax
from jax.experimental import pallas as pl
from jax.experimental.pallas import tpu as pltpu

F32 = jnp.float32
BF16 = jnp.bfloat16

D_MODEL = 1024
DEPTH = 1
PLE_DIM = 256
ROPE_THETA = 10000.0
LN_EPS = 1e-5
RMS_EPS = 1e-6
NEG_INF = -1e30

DIFF_HEADS = 8
DIFF_HEAD_DIM = 64
DIFF_QK_WIDTH = 2 * DIFF_HEADS * DIFF_HEAD_DIM
DIFF_WIDTH = DIFF_HEADS * 2 * DIFF_HEAD_DIM

MLA_HEADS = 8
MLA_Q_LORA = 384
MLA_KV_LORA = 256
MLA_NOPE = 128
MLA_ROPE = 64
MLA_V = 128
MLA_WIDTH = MLA_HEADS * MLA_V

DEEPNORM_ALPHA = (2 * DEPTH) ** 0.25
LAMBDA_INIT = 0.8 - 0.6 * math.exp(-0.3 * 0)

LANES = 128
HEAD_LANES = 2 * DIFF_HEAD_DIM
MLA_QK_PAD = 256
LOG2E = math.log2(math.e)
HALF_ROT = DIFF_HEAD_DIM // 2

KEY_CHUNK = 256
DIFF_ATTN_COLUMNS = 2048
MLA_ATTN_COLUMNS = 2048

OFF_KA = 0
OFF_ZA = OFF_KA + DIFF_QK_WIDTH
OFF_CQ = OFF_ZA + DIFF_WIDTH
OFF_CKV = OFF_CQ + MLA_Q_LORA
OFF_KPE = OFF_CKV + MLA_KV_LORA
OFF_ZB = OFF_KPE + LANES
OFF_G = OFF_ZB + MLA_WIDTH

VMEM_LIMIT_BYTES = 56 * 1024 * 1024


def _layer_norm(x, g, b):
    mu = jnp.mean(x, axis=-1, keepdims=True)
    xc = x - mu
    var = jnp.mean(xc * xc, axis=-1, keepdims=True)
    return xc * lax.rsqrt(var + LN_EPS) * g + b


def _rms_norm(x, g):
    return x * lax.rsqrt(jnp.mean(x * x, axis=-1, keepdims=True) + RMS_EPS) * g


def _dot(a, b):
    return jnp.dot(a, b, preferred_element_type=F32)


def _dot_nt(a, b):
    return lax.dot_general(a, b, (((1,), (1,)), ((), ())), preferred_element_type=F32)


def _proj_kernel(x_ref, pos_ref, inv_ref, lng_ref, lnb_ref, win_ref, wqat_ref, wvat_ref, bg_ref,
                 qng_ref, wuqt_ref, kvg_ref, wuk_ref, wuvt_ref,
                 qat_ref, ka_ref, vat_ref, za_ref, qbt_ref, kb_ref, vbt_ref, zb_ref, g_ref):
    tm = x_ref.shape[0]
    tk = qat_ref.shape[-1]
    xn = _layer_norm(x_ref[...], lng_ref[...], lnb_ref[...]).astype(BF16)

    ang = pos_ref[...].astype(F32) * inv_ref[...]
    cos = jnp.cos(ang)
    sin = jnp.sin(ang)
    lane = lax.broadcasted_iota(jnp.int32, (1, LANES), 1)
    first_half = (lane % DIFF_HEAD_DIM) < HALF_ROT
    sin_signed = jnp.where(first_half, -sin, sin)
    cos_t = cos.T
    sin_signed_t = sin_signed.T

    def rope(t):
        partner = jnp.where(first_half, pltpu.roll(t, LANES - HALF_ROT, 1), pltpu.roll(t, HALF_ROT, 1))
        return t * cos + partner * sin_signed

    def rope_t(t):
        h = HALF_ROT
        partner = jnp.concatenate([t[h:2 * h], t[0:h], t[3 * h:4 * h], t[2 * h:3 * h]], axis=0)
        return t * cos_t + partner * sin_signed_t

    def proj(off, width):
        return _dot(xn, win_ref[:, off:off + width])

    def store_chunks(ref, h, rows, t):
        for c in range(tm // tk):
            ref[h, c, rows, :] = t[:, c * tk:(c + 1) * tk].astype(BF16)

    all_rows = slice(None)
    qa_scale = DIFF_HEAD_DIM ** -0.5 * LOG2E
    hqt = _dot_nt(wqat_ref[...], xn)
    for h in range(DIFF_HEADS):
        store_chunks(qat_ref, h, all_rows, rope_t(hqt[h * HEAD_LANES:(h + 1) * HEAD_LANES]) * qa_scale)
    hk = proj(OFF_KA, DIFF_QK_WIDTH)
    for h in range(DIFF_HEADS):
        ka_ref[h] = rope(hk[:, h * HEAD_LANES:(h + 1) * HEAD_LANES]).astype(BF16)
    hvt = _dot_nt(wvat_ref[...], xn)
    for h in range(DIFF_HEADS):
        store_chunks(vat_ref, h, all_rows, hvt[h * HEAD_LANES:(h + 1) * HEAD_LANES])
    hz = proj(OFF_ZA, DIFF_WIDTH)
    za_ref[...] = (hz * jax.nn.sigmoid(hz)).astype(BF16)

    qb_scale = (MLA_NOPE + MLA_ROPE) ** -0.5 * LOG2E
    cq = _rms_norm(proj(OFF_CQ, MLA_Q_LORA), qng_ref[...]).astype(BF16)
    uqt = _dot_nt(wuqt_ref[...], cq)
    for h in range(MLA_HEADS):
        base = h * MLA_QK_PAD
        store_chunks(qbt_ref, h, slice(0, LANES), uqt[base:base + LANES] * qb_scale)
        store_chunks(qbt_ref, h, slice(LANES, 2 * LANES), rope_t(uqt[base + LANES:base + 2 * LANES]) * qb_scale)

    k_rot = rope(proj(OFF_KPE, LANES)).astype(BF16)
    ckv = _rms_norm(proj(OFF_CKV, MLA_KV_LORA), kvg_ref[...]).astype(BF16)
    uk = _dot(ckv, wuk_ref[...])
    for h in range(MLA_HEADS):
        kb_ref[h, :, 0:LANES] = uk[:, h * MLA_NOPE:(h + 1) * MLA_NOPE].astype(BF16)
        kb_ref[h, :, LANES:2 * LANES] = k_rot
    uvt = _dot_nt(wuvt_ref[...], ckv)
    for h in range(MLA_HEADS):
        store_chunks(vbt_ref, h, all_rows, uvt[h * MLA_V:(h + 1) * MLA_V])

    hzb = proj(OFF_ZB, MLA_WIDTH)
    zb_ref[...] = (hzb * jax.nn.sigmoid(hzb)).astype(BF16)
    g_ref[...] = jax.nn.sigmoid(proj(OFF_G, 2 * D_MODEL) + bg_ref[...]).astype(BF16)


def _resident(shape):
    nd = len(shape)
    return pl.BlockSpec(shape, lambda *_: (0,) * nd, pipeline_mode=pl.Buffered(1))


def _proj_call(x, pos, consts, *, tm, tk):
    B, S, D = x.shape
    grid = (B, S // tm)
    row = lambda b, i: (b, i, 0)
    head = lambda b, i: (b, 0, i, 0)
    head_t = lambda b, i: (b, 0, i, 0, 0)
    bf = lambda shape: jax.ShapeDtypeStruct(shape, BF16)
    nc = S // tk
    out_shape = [
        bf((B, DIFF_HEADS, nc, HEAD_LANES, tk)), bf((B, DIFF_HEADS, S, HEAD_LANES)),
        bf((B, DIFF_HEADS, nc, HEAD_LANES, tk)), bf((B, S, DIFF_WIDTH)),
        bf((B, MLA_HEADS, nc, MLA_QK_PAD, tk)), bf((B, MLA_HEADS, S, MLA_QK_PAD)),
        bf((B, MLA_HEADS, nc, MLA_V, tk)), bf((B, S, MLA_WIDTH)), bf((B, S, 2 * D)),
    ]
    out_specs = [
        pl.BlockSpec((None, DIFF_HEADS, tm // tk, HEAD_LANES, tk), head_t),
        pl.BlockSpec((None, DIFF_HEADS, tm, HEAD_LANES), head),
        pl.BlockSpec((None, DIFF_HEADS, tm // tk, HEAD_LANES, tk), head_t),
        pl.BlockSpec((None, tm, DIFF_WIDTH), row),
        pl.BlockSpec((None, MLA_HEADS, tm // tk, MLA_QK_PAD, tk), head_t),
        pl.BlockSpec((None, MLA_HEADS, tm, MLA_QK_PAD), head),
        pl.BlockSpec((None, MLA_HEADS, tm // tk, MLA_V, tk), head_t),
        pl.BlockSpec((None, tm, MLA_WIDTH), row),
        pl.BlockSpec((None, tm, 2 * D), row),
    ]
    in_specs = [pl.BlockSpec((None, tm, D), row), pl.BlockSpec((None, tm, 1), row)]
    in_specs += [_resident(c.shape) for c in consts]
    return pl.pallas_call(
        _proj_kernel, grid=grid, in_specs=in_specs, out_specs=out_specs, out_shape=out_shape,
        compiler_params=pltpu.CompilerParams(
            dimension_semantics=("parallel", "parallel"), vmem_limit_bytes=VMEM_LIMIT_BYTES),
        name="proj",
    )(x, pos, *consts)


SUM_ROWS = 16


class _AttnState:
    def __init__(self, qt_ref, s_ref, cmax_ref, m_ref, acc_ref):
        self.qt = qt_ref
        self.s = s_ref
        self.cmax = cmax_ref
        self.m = m_ref
        self.acc = acc_ref

    @staticmethod
    def scratch_shapes(dk, dv, tk, nq):
        return [pltpu.VMEM((dk, nq), BF16), pltpu.VMEM((2, tk, nq), F32), pltpu.VMEM((2, 1, nq), F32),
                pltpu.VMEM((1, nq), F32), pltpu.VMEM((dv + SUM_ROWS, nq), F32)]


def _causal_attention(st, k_ref, vt_ref, i, tk, maps):
    nq = st.qt.shape[1]
    dv = vt_ref.shape[1]
    group = maps * tk
    subtiles = nq // group
    assert subtiles % 2 == 0 or subtiles == 1
    n = i * subtiles
    ones = jnp.ones((SUM_ROWS, tk), BF16)
    visible = (lax.broadcasted_iota(jnp.int32, (tk, tk), 0) <= lax.broadcasted_iota(jnp.int32, (tk, tk), 1))

    def store_scores(slot, c0, s, boundary):
        if boundary:
            s = jnp.where(jnp.concatenate([visible] * (s.shape[1] // tk), axis=1), s, NEG_INF)
        st.s[slot, :, c0:c0 + s.shape[1]] = s
        st.cmax[slot, :, c0:c0 + s.shape[1]] = jnp.max(s, axis=0, keepdims=True)

    def scores(c, slot, c0=0, diagonal=False):
        k = k_ref[pl.ds(pl.multiple_of(c * tk, tk), tk), :]
        s = _dot(k, st.qt[:, c0:])
        if diagonal:
            store_scores(slot, c0, s[:, :group], True)
            if c0 + group < nq:
                store_scores(slot, c0 + group, s[:, group:], False)
        else:
            store_scores(slot, c0, s, False)

    def softmax_pv(c, slot, c0=0):
        m_prev = st.m[:, c0:]
        m_new = jnp.maximum(m_prev, st.cmax[slot, :, c0:])
        pt = jnp.exp2(st.s[slot, :, c0:] - m_new).astype(BF16)
        st.m[:, c0:] = m_new
        values = jnp.concatenate([vt_ref[c], ones], axis=0)
        st.acc[:, c0:] = jnp.exp2(m_prev - m_new) * st.acc[:, c0:] + _dot(values, pt)

    scores(0, 0)
    st.acc[...] = jnp.zeros_like(st.acc)
    st.m[...] = jnp.full_like(st.m, NEG_INF)

    def pair(t, carry):
        c = 2 * t
        scores(c + 1, 1)
        softmax_pv(c, 0)
        scores(c + 2, 0)
        softmax_pv(c + 1, 1)
        return carry

    lax.fori_loop(0, n // 2, pair, 0)

    store_scores(0, 0, st.s[0, :, 0:group], True)
    for r in range(1, subtiles):
        scores(n + r, r % 2, r * group, diagonal=True)
        softmax_pv(n + r - 1, (r - 1) % 2, (r - 1) * group)
    softmax_pv(n + subtiles - 1, (subtiles - 1) % 2, (subtiles - 1) * group)
    return st.acc[0:dv, :] * (1.0 / st.acc[dv:dv + 1, :])


def _diff_attn_kernel(qt_ref, k_ref, vt_ref, lam_ref, g_ref, o_ref, *scratch):
    i = pl.program_id(2)
    subtiles, dk, tk = qt_ref.shape
    st = _AttnState(*scratch)
    feature = lax.broadcasted_iota(jnp.int32, (dk, 1), 0)
    for s in range(subtiles):
        q = qt_ref[s]
        zero = jnp.zeros_like(q)
        st.qt[:, 2 * s * tk:(2 * s + 1) * tk] = jnp.where(feature < DIFF_HEAD_DIM, q, zero)
        st.qt[:, (2 * s + 1) * tk:(2 * s + 2) * tk] = jnp.where(feature >= DIFF_HEAD_DIM, q, zero)
    ot = _causal_attention(st, k_ref, vt_ref, i, tk, 2)

    lq = lam_ref[...]
    lam = (jnp.exp(jnp.sum(lq[0:1] * lq[1:2], axis=-1, keepdims=True))
           - jnp.exp(jnp.sum(lq[2:3] * lq[3:4], axis=-1, keepdims=True)) + LAMBDA_INIT)
    for s in range(subtiles):
        c = 2 * s * tk
        od = ot[:, c:c + tk] - lam * ot[:, c + tk:c + 2 * tk]
        od = od * lax.rsqrt(jnp.mean(od * od, axis=0, keepdims=True) + RMS_EPS)
        o_ref[s * tk:(s + 1) * tk, :] = (od.T * (g_ref[...] * (1.0 - LAMBDA_INIT))).astype(BF16)


def _mla_attn_kernel(qt_ref, k_ref, vt_ref, o_ref, *scratch):
    i = pl.program_id(2)
    subtiles, _, tk = qt_ref.shape
    st = _AttnState(*scratch)
    for s in range(subtiles):
        st.qt[:, s * tk:(s + 1) * tk] = qt_ref[s]
    ot = _causal_attention(st, k_ref, vt_ref, i, tk, 1)
    for s in range(subtiles):
        o_ref[s * tk:(s + 1) * tk, :] = ot[:, s * tk:(s + 1) * tk].T.astype(BF16)


def _attn_call(body, name, qt, k, vt, extra, *, maps, columns):
    B, H, nc, dk, tk = qt.shape
    S = k.shape[2]
    dv = vt.shape[3]
    subtiles = min(columns // (maps * tk), nc)
    tq = subtiles * tk
    assert S % tq == 0
    return pl.pallas_call(
        body,
        grid=(B, H, S // tq),
        in_specs=[
            pl.BlockSpec((None, None, subtiles, dk, tk), lambda b, h, i: (b, h, i, 0, 0)),
            pl.BlockSpec((None, None, S, dk), lambda b, h, i: (b, h, 0, 0)),
            pl.BlockSpec((None, None, nc, dv, tk), lambda b, h, i: (b, h, 0, 0, 0)),
        ] + [pl.BlockSpec(e.shape, lambda b, h, i: (0, 0)) for e in extra],
        out_specs=pl.BlockSpec((None, None, tq, dv), lambda b, h, i: (b, h, i, 0)),
        out_shape=jax.ShapeDtypeStruct((B, H, S, dv), BF16),
        scratch_shapes=_AttnState.scratch_shapes(dk, dv, tk, maps * tq),
        compiler_params=pltpu.CompilerParams(
            dimension_semantics=("parallel", "parallel", "arbitrary"), vmem_limit_bytes=VMEM_LIMIT_BYTES),
        name=name,
    )(qt, k, vt, *extra)


def _out_kernel(x_ref, lng_ref, lnb_ref, oa_ref, za_ref, ob_ref, zb_ref, g_ref, p_ref,
                woa_ref, wob_ref, wout_ref, wpg_ref, bpg_ref, wpp_ref, pog_ref, pob_ref, y_ref):
    xn = _layer_norm(x_ref[...], lng_ref[...], lnb_ref[...])
    oa = jnp.concatenate([oa_ref[h] for h in range(DIFF_HEADS)], axis=-1)
    ob = jnp.concatenate([ob_ref[h] for h in range(MLA_HEADS)], axis=-1)
    ya = _dot(oa * za_ref[...], woa_ref[...])
    yb = _dot(ob * zb_ref[...], wob_ref[...])
    g = g_ref[...]
    merged = g[:, :D_MODEL].astype(F32) * ya + g[:, D_MODEL:].astype(F32) * yb
    y = DEEPNORM_ALPHA * xn + _dot(merged.astype(BF16), wout_ref[...])
    gate = jax.nn.sigmoid(_dot(y.astype(BF16), wpg_ref[...]) + bpg_ref[...])
    y = y + gate * _dot(p_ref[...].astype(BF16), wpp_ref[...])
    y_ref[...] = _layer_norm(y, pog_ref[...], pob_ref[...])


def _out_call(x, lng, lnb, oa, za, ob, zb, g, p, woa, wob, wout, wpg, bpg, wpp, pog, pob, *, tm):
    B, S, D = x.shape
    row = lambda b, i: (b, i, 0)
    head = lambda b, i: (b, 0, i, 0)
    in_specs = [
        pl.BlockSpec((None, tm, D), row), _resident(lng.shape), _resident(lnb.shape),
        pl.BlockSpec((None, DIFF_HEADS, tm, HEAD_LANES), head),
        pl.BlockSpec((None, tm, DIFF_WIDTH), row),
        pl.BlockSpec((None, MLA_HEADS, tm, MLA_V), head),
        pl.BlockSpec((None, tm, MLA_WIDTH), row),
        pl.BlockSpec((None, tm, 2 * D), row),
        pl.BlockSpec((None, tm, PLE_DIM), row),
        _resident(woa.shape), _resident(wob.shape), _resident(wout.shape), _resident(wpg.shape),
        _resident(bpg.shape), _resident(wpp.shape), _resident(pog.shape), _resident(pob.shape),
    ]
    return pl.pallas_call(
        _out_kernel, grid=(B, S // tm), in_specs=in_specs,
        out_specs=pl.BlockSpec((None, tm, D), row),
        out_shape=jax.ShapeDtypeStruct((B, S, D), F32),
        compiler_params=pltpu.CompilerParams(
            dimension_semantics=("parallel", "parallel"), vmem_limit_bytes=VMEM_LIMIT_BYTES),
        name="merge_out",
    )(x, lng, lnb, oa, za, ob, zb, g, p, woa, wob, wout, wpg, bpg, wpp, pog, pob)


def kernel(x, p, positions, ln_emb_g, ln_emb_b, w_in, b_gate, diff_lambda, diff_subln_g, w_o_a,
           mla_q_norm_g, mla_w_uq, mla_kv_norm_g, mla_w_ukv, w_o_b, w_out, ple_w_gate, ple_b_gate,
           ple_w_proj, ln_post_g, ln_post_b):
    B, S, D = x.shape
    assert D == D_MODEL and w_in.shape[0] == DEPTH == 1
    tk = min(KEY_CHUNK, S)
    tm = tk
    assert S % tm == 0
    row2 = lambda v: v.reshape(1, -1)

    w = w_in[0].astype(BF16)
    ka0, va0, za0 = DIFF_QK_WIDTH, 2 * DIFF_QK_WIDTH, 2 * DIFF_QK_WIDTH + DIFF_WIDTH
    kpe_end = za0 + DIFF_WIDTH + MLA_Q_LORA + MLA_KV_LORA + MLA_ROPE
    win = jnp.concatenate([w[:, ka0:va0], w[:, za0:kpe_end],
                           jnp.zeros((D, LANES - MLA_ROPE), BF16), w[:, kpe_end:]], axis=1)
    wqat = w[:, :ka0].T
    wvat = w[:, va0:za0].T
    wuq = mla_w_uq[0].astype(BF16).reshape(MLA_Q_LORA, MLA_HEADS, MLA_NOPE + MLA_ROPE)
    wuqt = jnp.pad(wuq, ((0, 0), (0, 0), (0, MLA_QK_PAD - MLA_NOPE - MLA_ROPE))).reshape(MLA_Q_LORA, -1).T
    wukv = mla_w_ukv[0].astype(BF16).reshape(MLA_KV_LORA, MLA_HEADS, MLA_NOPE + MLA_V)
    wuk = wukv[:, :, :MLA_NOPE].reshape(MLA_KV_LORA, -1)
    wuvt = wukv[:, :, MLA_NOPE:].reshape(MLA_KV_LORA, -1).T
    inv = ROPE_THETA ** (-jnp.arange(0, DIFF_HEAD_DIM, 2, dtype=F32) / DIFF_HEAD_DIM)
    inv = jnp.tile(inv, LANES // inv.shape[0]).reshape(1, LANES)

    consts = (inv, row2(ln_emb_g), row2(ln_emb_b), win, wqat, wvat, row2(b_gate[0]),
              row2(mla_q_norm_g[0]), wuqt, row2(mla_kv_norm_g[0]), wuk, wuvt)
    qat, ka, vat, za, qbt, kb, vbt, zb, g = _proj_call(x, positions.reshape(B, S, 1), consts, tm=tm, tk=tk)
    oa = _attn_call(_diff_attn_kernel, "diff_attn", qat, ka, vat,
                    (diff_lambda[0], row2(diff_subln_g[0])), maps=2, columns=DIFF_ATTN_COLUMNS)
    ob = _attn_call(_mla_attn_kernel, "mla_attn", qbt, kb, vbt, (), maps=1, columns=MLA_ATTN_COLUMNS)
    return _out_call(
        x, row2(ln_emb_g), row2(ln_emb_b), oa, za, ob, zb, g, p[0],
        w_o_a[0].astype(BF16), w_o_b[0].astype(BF16), w_out[0].astype(BF16), ple_w_gate[0].astype(BF16),
        row2(ple_b_gate[0]), ple_w_proj[0].astype(BF16), row2(ln_post_g[0]), row2(ln_post_b[0]), tm=tm)
```

```python
import functools
import math

import jax
import jax.numpy as jnp
from jax import lax
from jax.experimental import pallas as pl
from jax.experimental.pallas import tpu as pltpu

F32 = jnp.float32
BF16 = jnp.bfloat16

D_MODEL = 1024
DEPTH = 1
PLE_DIM = 256
ROPE_THETA = 10000.0
LN_EPS = 1e-5
RMS_EPS = 1e-6
NEG_INF = -1e30

DIFF_HEADS = 8
DIFF_HEAD_DIM = 64
DIFF_QK_WIDTH = 2 * DIFF_HEADS * DIFF_HEAD_DIM
DIFF_WIDTH = DIFF_HEADS * 2 * DIFF_HEAD_DIM

MLA_HEADS = 8
MLA_Q_LORA = 384
MLA_KV_LORA = 256
MLA_NOPE = 128
MLA_ROPE = 64
MLA_V = 128
MLA_WIDTH = MLA_HEADS * MLA_V

DEEPNORM_ALPHA = (2 * DEPTH) ** 0.25
LAMBDA_INIT = 0.8 - 0.6 * math.exp(-0.3 * 0)

LANES = 128
HEAD_LANES = 2 * DIFF_HEAD_DIM
MLA_QK_PAD = 256
LOG2E = math.log2(math.e)
HALF_ROT = DIFF_HEAD_DIM // 2

KEY_CHUNK = 256
DIFF_ATTN_COLUMNS = 4096
MLA_ATTN_COLUMNS = 2048
MERGE_SUBTILE = 256
MERGE_ROWS = 2 * MERGE_SUBTILE

OFF_KA = 0
OFF_ZA = OFF_KA + DIFF_QK_WIDTH
OFF_CQ = OFF_ZA + DIFF_WIDTH
OFF_CKV = OFF_CQ + MLA_Q_LORA
OFF_KPE = OFF_CKV + MLA_KV_LORA
OFF_ZB = OFF_KPE + LANES
OFF_G = OFF_ZB + MLA_WIDTH

VMEM_LIMIT_BYTES = 56 * 1024 * 1024


def _layer_norm(x, g, b):
    mu = jnp.mean(x, axis=-1, keepdims=True)
    xc = x - mu
    var = jnp.mean(xc * xc, axis=-1, keepdims=True)
    return xc * lax.rsqrt(var + LN_EPS) * g + b


def _rms_norm(x, g):
    return x * lax.rsqrt(jnp.mean(x * x, axis=-1, keepdims=True) + RMS_EPS) * g


def _dot(a, b):
    return jnp.dot(a, b, preferred_element_type=F32)


def _dot_nt(a, b):
    return lax.dot_general(a, b, (((1,), (1,)), ((), ())), preferred_element_type=F32)


def _proj_kernel(x_ref, pos_ref, inv_ref, lng_ref, lnb_ref, win_ref, wqat_ref, wvat_ref, bg_ref,
                 qng_ref, wuqt_ref, kvg_ref, wuk_ref, wuvt_ref,
                 qat_ref, ka_ref, vat_ref, za_ref, qbt_ref, kb_ref, vbt_ref, zb_ref, g_ref):
    tm = x_ref.shape[0]
    tk = qat_ref.shape[-1]
    xn = _layer_norm(x_ref[...], lng_ref[...], lnb_ref[...]).astype(BF16)

    ang = pos_ref[...].astype(F32) * inv_ref[...]
    cos = jnp.cos(ang)
    sin = jnp.sin(ang)
    lane = lax.broadcasted_iota(jnp.int32, (1, LANES), 1)
    first_half = (lane % DIFF_HEAD_DIM) < HALF_ROT
    sin_signed = jnp.where(first_half, -sin, sin)
    cos_t = cos.T
    sin_signed_t = sin_signed.T

    def rope(t):
        partner = jnp.where(first_half, pltpu.roll(t, LANES - HALF_ROT, 1), pltpu.roll(t, HALF_ROT, 1))
        return t * cos + partner * sin_signed

    def rope_t(t):
        h = HALF_ROT
        halves = [t[j * h:(j + 1) * h] for j in range(t.shape[0] // h)]
        partner = jnp.concatenate([halves[j ^ 1] for j in range(len(halves))], axis=0)
        return t * cos_t[:t.shape[0]] + partner * sin_signed_t[:t.shape[0]]

    def proj(off, width):
        return _dot(xn, win_ref[:, off:off + width])

    def store_chunks(ref, h, rows, t):
        for c in range(tm // tk):
            ref[h, c, rows, :] = t[:, c * tk:(c + 1) * tk].astype(BF16)

    g_ref[...] = jax.nn.sigmoid(proj(OFF_G, 2 * D_MODEL) + bg_ref[...]).astype(BF16)
    hzb = proj(OFF_ZB, MLA_WIDTH)
    zb_ref[...] = (hzb * jax.nn.sigmoid(hzb)).astype(BF16)
    hz = proj(OFF_ZA, DIFF_WIDTH)
    za_ref[...] = (hz * jax.nn.sigmoid(hz)).astype(BF16)
    all_rows = slice(None)
    hvt = _dot_nt(wvat_ref[...], xn)
    for h in range(DIFF_HEADS):
        store_chunks(vat_ref, h, all_rows, hvt[h * HEAD_LANES:(h + 1) * HEAD_LANES])

    latent = proj(OFF_CQ, OFF_ZB - OFF_CQ)
    cq = _rms_norm(latent[:, :MLA_Q_LORA], qng_ref[...]).astype(BF16)
    ckv = _rms_norm(latent[:, MLA_Q_LORA:MLA_Q_LORA + MLA_KV_LORA], kvg_ref[...]).astype(BF16)
    uvt = _dot_nt(wuvt_ref[...], ckv)
    for h in range(MLA_HEADS):
        store_chunks(vbt_ref, h, all_rows, uvt[h * MLA_V:(h + 1) * MLA_V])

    qa_scale = DIFF_HEAD_DIM ** -0.5 * LOG2E
    hqt = _dot_nt(wqat_ref[...], xn)
    for h in range(DIFF_HEADS):
        store_chunks(qat_ref, h, all_rows, rope_t(hqt[h * HEAD_LANES:(h + 1) * HEAD_LANES]) * qa_scale)
    hk = proj(OFF_KA, DIFF_QK_WIDTH)
    for h in range(DIFF_HEADS):
        ka_ref[h] = rope(hk[:, h * HEAD_LANES:(h + 1) * HEAD_LANES]).astype(BF16)

    qb_scale = (MLA_NOPE + MLA_ROPE) ** -0.5 * LOG2E
    uqt = _dot_nt(wuqt_ref[...], cq)
    pe0 = MLA_HEADS * MLA_NOPE
    for h in range(MLA_HEADS):
        store_chunks(qbt_ref, h, slice(0, MLA_NOPE), uqt[h * MLA_NOPE:(h + 1) * MLA_NOPE] * qb_scale)
        store_chunks(qbt_ref, h, slice(MLA_NOPE, MLA_NOPE + MLA_ROPE),
                     rope_t(uqt[pe0 + h * MLA_ROPE:pe0 + (h + 1) * MLA_ROPE]) * qb_scale)
        store_chunks(qbt_ref, h, slice(MLA_NOPE + MLA_ROPE, MLA_QK_PAD),
                     jnp.zeros((MLA_QK_PAD - MLA_NOPE - MLA_ROPE, tm), F32))

    k_rot = rope(latent[:, MLA_Q_LORA + MLA_KV_LORA:]).astype(BF16)
    uk = _dot(ckv, wuk_ref[...])
    for h in range(MLA_HEADS):
        kb_ref[h, :, 0:LANES] = uk[:, h * MLA_NOPE:(h + 1) * MLA_NOPE].astype(BF16)
        kb_ref[h, :, LANES:2 * LANES] = k_rot


def _resident(shape):
    nd = len(shape)
    return pl.BlockSpec(shape, lambda *_: (0,) * nd, pipeline_mode=pl.Buffered(1))


def _proj_call(x, pos, consts, *, tm, tk):
    B, S, D = x.shape
    grid = (B, S // tm)
    row = lambda b, i: (b, i, 0)
    head = lambda b, i: (b, 0, i, 0)
    head_t = lambda b, i: (b, 0, i, 0, 0)
    bf = lambda shape: jax.ShapeDtypeStruct(shape, BF16)
    nc = S // tk
    out_shape = [
        bf((B, DIFF_HEADS, nc, HEAD_LANES, tk)), bf((B, DIFF_HEADS, S, HEAD_LANES)),
        bf((B, DIFF_HEADS, nc, HEAD_LANES, tk)), bf((B, S, DIFF_WIDTH)),
        bf((B, MLA_HEADS, nc, MLA_QK_PAD, tk)), bf((B, MLA_HEADS, S, MLA_QK_PAD)),
        bf((B, MLA_HEADS, nc, MLA_V, tk)), bf((B, S, MLA_WIDTH)), bf((B, S, 2 * D)),
    ]
    out_specs = [
        pl.BlockSpec((None, DIFF_HEADS, tm // tk, HEAD_LANES, tk), head_t),
        pl.BlockSpec((None, DIFF_HEADS, tm, HEAD_LANES), head),
        pl.BlockSpec((None, DIFF_HEADS, tm // tk, HEAD_LANES, tk), head_t),
        pl.BlockSpec((None, tm, DIFF_WIDTH), row),
        pl.BlockSpec((None, MLA_HEADS, tm // tk, MLA_QK_PAD, tk), head_t),
        pl.BlockSpec((None, MLA_HEADS, tm, MLA_QK_PAD), head),
        pl.BlockSpec((None, MLA_HEADS, tm // tk, MLA_V, tk), head_t),
        pl.BlockSpec((None, tm, MLA_WIDTH), row),
        pl.BlockSpec((None, tm, 2 * D), row),
    ]
    in_specs = [pl.BlockSpec((None, tm, D), row), pl.BlockSpec((None, tm, 1), row)]
    in_specs += [_resident(c.shape) for c in consts]
    return pl.pallas_call(
        _proj_kernel, grid=grid, in_specs=in_specs, out_specs=out_specs, out_shape=out_shape,
        compiler_params=pltpu.CompilerParams(
            dimension_semantics=("parallel", "parallel"), vmem_limit_bytes=VMEM_LIMIT_BYTES),
        name="proj",
    )(x, pos, *consts)


SUM_ROWS = 16


class _AttnState:
    def __init__(self, qt_ref, s_ref, cmax_ref, m_ref, acc_ref):
        self.qt = qt_ref
        self.s = s_ref
        self.cmax = cmax_ref
        self.m = m_ref
        self.acc = acc_ref

    @staticmethod
    def scratch_shapes(dk, dv, tk, nq):
        return [pltpu.VMEM((dk, nq), BF16), pltpu.VMEM((2, tk, nq), F32), pltpu.VMEM((2, 1, nq), F32),
                pltpu.VMEM((1, nq), F32), pltpu.VMEM((dv + SUM_ROWS, nq), F32)]


def _causal_attention(st, k_ref, vt_ref, i, tk, maps):
    nq = st.qt.shape[1]
    dv = vt_ref.shape[1]
    group = maps * tk
    subtiles = nq // group
    assert subtiles % 2 == 0 or subtiles == 1
    n = i * subtiles
    ones = jnp.ones((SUM_ROWS, tk), BF16)
    visible = (lax.broadcasted_iota(jnp.int32, (tk, tk), 0) <= lax.broadcasted_iota(jnp.int32, (tk, tk), 1))

    def store_scores(slot, c0, s, boundary):
        if boundary:
            s = jnp.where(jnp.concatenate([visible] * (s.shape[1] // tk), axis=1), s, NEG_INF)
        st.s[slot, :, c0:c0 + s.shape[1]] = s
        st.cmax[slot, :, c0:c0 + s.shape[1]] = jnp.max(s, axis=0, keepdims=True)

    def scores(c, slot, c0=0, diagonal=False):
        k = k_ref[pl.ds(pl.multiple_of(c * tk, tk), tk), :]
        s = _dot(k, st.qt[:, c0:])
        if diagonal:
            store_scores(slot, c0, s[:, :group], True)
            if c0 + group < nq:
                store_scores(slot, c0 + group, s[:, group:], False)
        else:
            store_scores(slot, c0, s, False)

    def softmax_pv(c, slot, c0=0):
        m_prev = st.m[:, c0:]
        m_new = jnp.maximum(m_prev, st.cmax[slot, :, c0:])
        pt = jnp.exp2(st.s[slot, :, c0:] - m_new).astype(BF16)
        st.m[:, c0:] = m_new
        values = jnp.concatenate([vt_ref[c], ones], axis=0)
        st.acc[:, c0:] = jnp.exp2(m_prev - m_new) * st.acc[:, c0:] + _dot(values, pt)

    scores(0, 0)
    st.acc[...] = jnp.zeros_like(st.acc)
    st.m[...] = jnp.full_like(st.m, NEG_INF)

    def key_tile(t, carry):
        for u in range(subtiles):
            c = t * subtiles + u
            scores(c + 1, (u + 1) % 2)
            softmax_pv(c, u % 2)
        return carry

    lax.fori_loop(0, i, key_tile, 0)

    store_scores(0, 0, st.s[0, :, 0:group], True)
    for r in range(1, subtiles):
        scores(n + r, r % 2, r * group, diagonal=True)
        softmax_pv(n + r - 1, (r - 1) % 2, (r - 1) * group)
    softmax_pv(n + subtiles - 1, (subtiles - 1) % 2, (subtiles - 1) * group)
    return st.acc[0:dv, :] * (1.0 / st.acc[dv:dv + 1, :])


def _diff_attn_kernel(qt_ref, k_ref, vt_ref, lam_ref, g_ref, o_ref, *scratch):
    i = pl.program_id(2)
    subtiles, dk, tk = qt_ref.shape
    st = _AttnState(*scratch)
    feature = lax.broadcasted_iota(jnp.int32, (dk, 1), 0)
    for s in range(subtiles):
        q = qt_ref[s]
        zero = jnp.zeros_like(q)
        st.qt[:, 2 * s * tk:(2 * s + 1) * tk] = jnp.where(feature < DIFF_HEAD_DIM, q, zero)
        st.qt[:, (2 * s + 1) * tk:(2 * s + 2) * tk] = jnp.where(feature >= DIFF_HEAD_DIM, q, zero)
    ot = _causal_attention(st, k_ref, vt_ref, i, tk, 2)

    lq = lam_ref[...]
    lam = (jnp.exp(jnp.sum(lq[0:1] * lq[1:2], axis=-1, keepdims=True))
           - jnp.exp(jnp.sum(lq[2:3] * lq[3:4], axis=-1, keepdims=True)) + LAMBDA_INIT)
    for s in range(subtiles):
        c = 2 * s * tk
        od = ot[:, c:c + tk] - lam * ot[:, c + tk:c + 2 * tk]
        od = od * lax.rsqrt(jnp.mean(od * od, axis=0, keepdims=True) + RMS_EPS)
        o_ref[s * tk:(s + 1) * tk, :] = (od.T * (g_ref[...] * (1.0 - LAMBDA_INIT))).astype(BF16)


def _mla_attn_kernel(qt_ref, k_ref, vt_ref, o_ref, *scratch):
    i = pl.program_id(2)
    subtiles, _, tk = qt_ref.shape
    st = _AttnState(*scratch)
    for s in range(subtiles):
        st.qt[:, s * tk:(s + 1) * tk] = qt_ref[s]
    ot = _causal_attention(st, k_ref, vt_ref, i, tk, 1)
    for s in range(subtiles):
        o_ref[s * tk:(s + 1) * tk, :] = ot[:, s * tk:(s + 1) * tk].T.astype(BF16)


def _attn_call(body, name, qt, k, vt, extra, *, maps, columns):
    B, H, nc, dk, tk = qt.shape
    S = k.shape[2]
    dv = vt.shape[3]
    subtiles = min(columns // (maps * tk), nc)
    tq = subtiles * tk
    assert S % tq == 0
    return pl.pallas_call(
        body,
        grid=(B, H, S // tq),
        in_specs=[
            pl.BlockSpec((None, None, subtiles, dk, tk), lambda b, h, i: (b, h, i, 0, 0)),
            pl.BlockSpec((None, None, S, dk), lambda b, h, i: (b, h, 0, 0)),
            pl.BlockSpec((None, None, nc, dv, tk), lambda b, h, i: (b, h, 0, 0, 0)),
        ] + [pl.BlockSpec(e.shape, lambda b, h, i: (0, 0)) for e in extra],
        out_specs=pl.BlockSpec((None, None, tq, dv), lambda b, h, i: (b, h, i, 0)),
        out_shape=jax.ShapeDtypeStruct((B, H, S, dv), BF16),
        scratch_shapes=_AttnState.scratch_shapes(dk, dv, tk, maps * tq),
        compiler_params=pltpu.CompilerParams(
            dimension_semantics=("parallel", "parallel", "arbitrary"), vmem_limit_bytes=VMEM_LIMIT_BYTES),
        name=name,
    )(qt, k, vt, *extra)


def _out_kernel(x_ref, lng_ref, lnb_ref, oa_ref, za_ref, ob_ref, zb_ref, g_ref, p_ref,
                woa_ref, wob_ref, wout_ref, wpg_ref, bpg_ref, wpp_ref, pog_ref, pob_ref, y_ref):
    for r0 in range(0, x_ref.shape[0], MERGE_SUBTILE):
        rows = slice(r0, r0 + MERGE_SUBTILE)
        xn = _layer_norm(x_ref[rows, :], lng_ref[...], lnb_ref[...])
        oa = jnp.concatenate([oa_ref[h, rows, :] for h in range(DIFF_HEADS)], axis=-1)
        ob = jnp.concatenate([ob_ref[h, rows, :] for h in range(MLA_HEADS)], axis=-1)
        ya = _dot(oa * za_ref[rows, :], woa_ref[...])
        yb = _dot(ob * zb_ref[rows, :], wob_ref[...])
        g = g_ref[rows, :]
        merged = g[:, :D_MODEL].astype(F32) * ya + g[:, D_MODEL:].astype(F32) * yb
        y = DEEPNORM_ALPHA * xn + _dot(merged.astype(BF16), wout_ref[...])
        gate = jax.nn.sigmoid(_dot(y.astype(BF16), wpg_ref[...]) + bpg_ref[...])
        y = y + gate * _dot(p_ref[rows, :].astype(BF16), wpp_ref[...])
        y_ref[rows, :] = _layer_norm(y, pog_ref[...], pob_ref[...])


def _out_call(x, lng, lnb, oa, za, ob, zb, g, p, woa, wob, wout, wpg, bpg, wpp, pog, pob, *, tm):
    B, S, D = x.shape
    row = lambda b, i: (b, i, 0)
    head = lambda b, i: (b, 0, i, 0)
    in_specs = [
        pl.BlockSpec((None, tm, D), row), _resident(lng.shape), _resident(lnb.shape),
        pl.BlockSpec((None, DIFF_HEADS, tm, HEAD_LANES), head),
        pl.BlockSpec((None, tm, DIFF_WIDTH), row),
        pl.BlockSpec((None, MLA_HEADS, tm, MLA_V), head),
        pl.BlockSpec((None, tm, MLA_WIDTH), row),
        pl.BlockSpec((None, tm, 2 * D), row),
        pl.BlockSpec((None, tm, PLE_DIM), row),
        _resident(woa.shape), _resident(wob.shape), _resident(wout.shape), _resident(wpg.shape),
        _resident(bpg.shape), _resident(wpp.shape), _resident(pog.shape), _resident(pob.shape),
    ]
    return pl.pallas_call(
        _out_kernel, grid=(B, S // tm), in_specs=in_specs,
        out_specs=pl.BlockSpec((None, tm, D), row),
        out_shape=jax.ShapeDtypeStruct((B, S, D), F32),
        compiler_params=pltpu.CompilerParams(
            dimension_semantics=("parallel", "parallel"), vmem_limit_bytes=VMEM_LIMIT_BYTES),
        name="merge_out",
    )(x, lng, lnb, oa, za, ob, zb, g, p, woa, wob, wout, wpg, bpg, wpp, pog, pob)


def kernel(x, p, positions, ln_emb_g, ln_emb_b, w_in, b_gate, diff_lambda, diff_subln_g, w_o_a,
           mla_q_norm_g, mla_w_uq, mla_kv_norm_g, mla_w_ukv, w_o_b, w_out, ple_w_gate, ple_b_gate,
           ple_w_proj, ln_post_g, ln_post_b):
    B, S, D = x.shape
    assert D == D_MODEL and w_in.shape[0] == DEPTH == 1
    tk = min(KEY_CHUNK, S)
    tm = tk
    assert S % tm == 0
    row2 = lambda v: v.reshape(1, -1)

    w = w_in[0].astype(BF16)
    ka0, va0, za0 = DIFF_QK_WIDTH, 2 * DIFF_QK_WIDTH, 2 * DIFF_QK_WIDTH + DIFF_WIDTH
    kpe_end = za0 + DIFF_WIDTH + MLA_Q_LORA + MLA_KV_LORA + MLA_ROPE
    win = jnp.concatenate([w[:, ka0:va0], w[:, za0:kpe_end],
                           jnp.zeros((D, LANES - MLA_ROPE), BF16), w[:, kpe_end:]], axis=1)
    wqat = w[:, :ka0].T
    wvat = w[:, va0:za0].T
    wuq = mla_w_uq[0].astype(BF16).reshape(MLA_Q_LORA, MLA_HEADS, MLA_NOPE + MLA_ROPE)
    wuqt = jnp.concatenate([wuq[:, :, :MLA_NOPE].reshape(MLA_Q_LORA, -1),
                            wuq[:, :, MLA_NOPE:].reshape(MLA_Q_LORA, -1)], axis=1).T
    wukv = mla_w_ukv[0].astype(BF16).reshape(MLA_KV_LORA, MLA_HEADS, MLA_NOPE + MLA_V)
    wuk = wukv[:, :, :MLA_NOPE].reshape(MLA_KV_LORA, -1)
    wuvt = wukv[:, :, MLA_NOPE:].reshape(MLA_KV_LORA, -1).T
    inv = ROPE_THETA ** (-jnp.arange(0, DIFF_HEAD_DIM, 2, dtype=F32) / DIFF_HEAD_DIM)
    inv = jnp.tile(inv, LANES // inv.shape[0]).reshape(1, LANES)

    consts = (inv, row2(ln_emb_g), row2(ln_emb_b), win, wqat, wvat, row2(b_gate[0]),
              row2(mla_q_norm_g[0]), wuqt, row2(mla_kv_norm_g[0]), wuk, wuvt)
    qat, ka, vat, za, qbt, kb, vbt, zb, g = _proj_call(x, positions.reshape(B, S, 1), consts, tm=tm, tk=tk)
    oa = _attn_call(_diff_attn_kernel, "diff_attn", qat, ka, vat,
                    (diff_lambda[0], row2(diff_subln_g[0])), maps=2, columns=DIFF_ATTN_COLUMNS)
    ob = _attn_call(_mla_attn_kernel, "mla_attn", qbt, kb, vbt, (), maps=1, columns=MLA_ATTN_COLUMNS)
    return _out_call(
        x, row2(ln_emb_g), row2(ln_emb_b), oa, za, ob, zb, g, p[0],
        w_o_a[0].astype(BF16), w_o_b[0].astype(BF16), w_out[0].astype(BF16), ple_w_gate[0].astype(BF16),
        row2(ple_b_gate[0]), ple_w_proj[0].astype(BF16), row2(ln_post_g[0]), row2(ln_post_b[0]), tm=min(MERGE_ROWS, S))
```

```python
import functools
import math

import jax
import jax.numpy as jnp
from jax import lax
from jax.experimental import pallas as pl
from jax.experimental.pallas import tpu as pltpu

F32 = jnp.float32
BF16 = jnp.bfloat16

D_MODEL = 1024
DEPTH = 1
PLE_DIM = 256
ROPE_THETA = 10000.0
LN_EPS = 1e-5
RMS_EPS = 1e-6
NEG_INF = -1e30

DIFF_HEADS = 8
DIFF_HEAD_DIM = 64
DIFF_QK_WIDTH = 2 * DIFF_HEADS * DIFF_HEAD_DIM
DIFF_WIDTH = DIFF_HEADS * 2 * DIFF_HEAD_DIM

MLA_HEADS = 8
MLA_Q_LORA = 384
MLA_KV_LORA = 256
MLA_NOPE = 128
MLA_ROPE = 64
MLA_V = 128
MLA_WIDTH = MLA_HEADS * MLA_V

DEEPNORM_ALPHA = (2 * DEPTH) ** 0.25
LAMBDA_INIT = 0.8 - 0.6 * math.exp(-0.3 * 0)

LANES = 128
HEAD_LANES = 2 * DIFF_HEAD_DIM
MLA_QK_PAD = 256
LOG2E = math.log2(math.e)
HALF_ROT = DIFF_HEAD_DIM // 2

KEY_CHUNK = 256
DIFF_ATTN_COLUMNS = 4096
MLA_ATTN_COLUMNS = 2048
PROJ_ROWS = KEY_CHUNK
MERGE_SUBTILE = 256
MERGE_ROWS = 2 * MERGE_SUBTILE

OFF_KA = 0
OFF_ZA = OFF_KA + DIFF_QK_WIDTH
OFF_CQ = OFF_ZA + DIFF_WIDTH
OFF_CKV = OFF_CQ + MLA_Q_LORA
OFF_KPE = OFF_CKV + MLA_KV_LORA
OFF_ZB = OFF_KPE + LANES
OFF_G = OFF_ZB + MLA_WIDTH

VMEM_LIMIT_BYTES = 56 * 1024 * 1024


def _layer_norm(x, g, b):
    mu = jnp.mean(x, axis=-1, keepdims=True)
    xc = x - mu
    var = jnp.mean(xc * xc, axis=-1, keepdims=True)
    return xc * lax.rsqrt(var + LN_EPS) * g + b


def _rms_norm(x, g):
    return x * lax.rsqrt(jnp.mean(x * x, axis=-1, keepdims=True) + RMS_EPS) * g


def _dot(a, b):
    return jnp.dot(a, b, preferred_element_type=F32)


def _dot_nt(a, b):
    return lax.dot_general(a, b, (((1,), (1,)), ((), ())), preferred_element_type=F32)


def _proj_kernel(x_ref, pos_ref, inv_ref, lng_ref, lnb_ref, win_ref, wqat_ref, wvat_ref, bg_ref,
                 qng_ref, wuqt_ref, kvg_ref, wuk_ref, wuvt_ref,
                 qat_ref, ka_ref, vat_ref, za_ref, qbt_ref, kb_ref, vbt_ref, zb_ref, g_ref):
    tk = qat_ref.shape[-1]
    for r in range(x_ref.shape[0] // tk):
        _proj_rows(r, slice(r * tk, (r + 1) * tk), x_ref, pos_ref, inv_ref, lng_ref, lnb_ref, win_ref,
                   wqat_ref, wvat_ref, bg_ref, qng_ref, wuqt_ref, kvg_ref, wuk_ref, wuvt_ref,
                   qat_ref, ka_ref, vat_ref, za_ref, qbt_ref, kb_ref, vbt_ref, zb_ref, g_ref)


def _proj_rows(r, rows, x_ref, pos_ref, inv_ref, lng_ref, lnb_ref, win_ref, wqat_ref, wvat_ref, bg_ref,
               qng_ref, wuqt_ref, kvg_ref, wuk_ref, wuvt_ref,
               qat_ref, ka_ref, vat_ref, za_ref, qbt_ref, kb_ref, vbt_ref, zb_ref, g_ref):
    xn = _layer_norm(x_ref[rows, :], lng_ref[...], lnb_ref[...]).astype(BF16)

    lane = lax.broadcasted_iota(jnp.int32, (1, LANES), 1)
    group = lane // HALF_ROT
    ngroups = LANES // HALF_ROT
    quarter = (rows.stop - rows.start) // ngroups
    pos = pos_ref[rows, :].astype(F32)
    ang = pos[0:quarter] * inv_ref[...]
    for j in range(1, ngroups):
        ang = jnp.where(group == j, pos[j * quarter:(j + 1) * quarter] * inv_ref[...], ang)

    def replicate(compact):
        parts = []
        for j in range(ngroups):
            one = jnp.where(group == j, compact, 0.0)
            two = one + pltpu.roll(one, 2 * HALF_ROT, 1)
            parts.append(two + pltpu.roll(two, HALF_ROT, 1))
        return jnp.concatenate(parts, axis=0)

    cos = replicate(jnp.cos(ang))
    sin = replicate(jnp.sin(ang))
    first_half = (lane % DIFF_HEAD_DIM) < HALF_ROT
    sin_signed = jnp.where(first_half, -sin, sin)
    cos_t = cos.T
    sin_signed_t = sin_signed.T

    def rope(t):
        partner = jnp.where(first_half, pltpu.roll(t, LANES - HALF_ROT, 1), pltpu.roll(t, HALF_ROT, 1))
        return t * cos + partner * sin_signed

    def rope_t(t):
        h = HALF_ROT
        halves = [t[j * h:(j + 1) * h] for j in range(t.shape[0] // h)]
        partner = jnp.concatenate([halves[j ^ 1] for j in range(len(halves))], axis=0)
        return t * cos_t[:t.shape[0]] + partner * sin_signed_t[:t.shape[0]]

    def proj(off, width):
        return _dot(xn, win_ref[:, off:off + width])

    g_ref[rows, :] = jax.nn.sigmoid(proj(OFF_G, 2 * D_MODEL) + bg_ref[...]).astype(BF16)
    hzb = proj(OFF_ZB, MLA_WIDTH)
    zb_ref[rows, :] = (hzb * jax.nn.sigmoid(hzb)).astype(BF16)
    hz = proj(OFF_ZA, DIFF_WIDTH)
    za_ref[rows, :] = (hz * jax.nn.sigmoid(hz)).astype(BF16)
    hvt = _dot_nt(wvat_ref[...], xn)
    for h in range(DIFF_HEADS):
        vat_ref[h, r] = hvt[h * HEAD_LANES:(h + 1) * HEAD_LANES].astype(BF16)

    latent = proj(OFF_CQ, OFF_ZB - OFF_CQ)
    cq = _rms_norm(latent[:, :MLA_Q_LORA], qng_ref[...]).astype(BF16)
    ckv = _rms_norm(latent[:, MLA_Q_LORA:MLA_Q_LORA + MLA_KV_LORA], kvg_ref[...]).astype(BF16)
    uvt = _dot_nt(wuvt_ref[...], ckv)
    for h in range(MLA_HEADS):
        vbt_ref[h, r] = uvt[h * MLA_V:(h + 1) * MLA_V].astype(BF16)

    qa_scale = DIFF_HEAD_DIM ** -0.5 * LOG2E
    hqt = _dot_nt(wqat_ref[...], xn)
    for h in range(DIFF_HEADS):
        qat_ref[h, r] = (rope_t(hqt[h * HEAD_LANES:(h + 1) * HEAD_LANES]) * qa_scale).astype(BF16)
    hk = proj(OFF_KA, DIFF_QK_WIDTH)
    for h in range(DIFF_HEADS):
        ka_ref[h, rows, :] = rope(hk[:, h * HEAD_LANES:(h + 1) * HEAD_LANES]).astype(BF16)

    qb_scale = (MLA_NOPE + MLA_ROPE) ** -0.5 * LOG2E
    uqt = _dot_nt(wuqt_ref[...], cq)
    pe0 = MLA_HEADS * MLA_NOPE
    pe_end = MLA_NOPE + MLA_ROPE
    for h in range(MLA_HEADS):
        qbt_ref[h, r, 0:MLA_NOPE, :] = (uqt[h * MLA_NOPE:(h + 1) * MLA_NOPE] * qb_scale).astype(BF16)
        qbt_ref[h, r, MLA_NOPE:pe_end, :] = (
            rope_t(uqt[pe0 + h * MLA_ROPE:pe0 + (h + 1) * MLA_ROPE]) * qb_scale).astype(BF16)
        qbt_ref[h, r, pe_end:MLA_QK_PAD, :] = jnp.zeros((MLA_QK_PAD - pe_end, uqt.shape[1]), BF16)

    k_rot = rope(latent[:, MLA_Q_LORA + MLA_KV_LORA:]).astype(BF16)
    uk = _dot(ckv, wuk_ref[...])
    for h in range(MLA_HEADS):
        kb_ref[h, rows, 0:LANES] = uk[:, h * MLA_NOPE:(h + 1) * MLA_NOPE].astype(BF16)
        kb_ref[h, rows, LANES:2 * LANES] = k_rot


def _resident(shape):
    nd = len(shape)
    return pl.BlockSpec(shape, lambda *_: (0,) * nd, pipeline_mode=pl.Buffered(1))


def _proj_call(x, pos, consts, *, tm, tk):
    B, S, D = x.shape
    grid = (B, S // tm)
    row = lambda b, i: (b, i, 0)
    head = lambda b, i: (b, 0, i, 0)
    head_t = lambda b, i: (b, 0, i, 0, 0)
    bf = lambda shape: jax.ShapeDtypeStruct(shape, BF16)
    nc = S // tk
    out_shape = [
        bf((B, DIFF_HEADS, nc, HEAD_LANES, tk)), bf((B, DIFF_HEADS, S, HEAD_LANES)),
        bf((B, DIFF_HEADS, nc, HEAD_LANES, tk)), bf((B, S, DIFF_WIDTH)),
        bf((B, MLA_HEADS, nc, MLA_QK_PAD, tk)), bf((B, MLA_HEADS, S, MLA_QK_PAD)),
        bf((B, MLA_HEADS, nc, MLA_V, tk)), bf((B, S, MLA_WIDTH)), bf((B, S, 2 * D)),
    ]
    out_specs = [
        pl.BlockSpec((None, DIFF_HEADS, tm // tk, HEAD_LANES, tk), head_t),
        pl.BlockSpec((None, DIFF_HEADS, tm, HEAD_LANES), head),
        pl.BlockSpec((None, DIFF_HEADS, tm // tk, HEAD_LANES, tk), head_t),
        pl.BlockSpec((None, tm, DIFF_WIDTH), row),
        pl.BlockSpec((None, MLA_HEADS, tm // tk, MLA_QK_PAD, tk), head_t),
        pl.BlockSpec((None, MLA_HEADS, tm, MLA_QK_PAD), head),
        pl.BlockSpec((None, MLA_HEADS, tm // tk, MLA_V, tk), head_t),
        pl.BlockSpec((None, tm, MLA_WIDTH), row),
        pl.BlockSpec((None, tm, 2 * D), row),
    ]
    in_specs = [pl.BlockSpec((None, tm, D), row), pl.BlockSpec((None, tm, 1), row)]
    in_specs += [_resident(c.shape) for c in consts]
    return pl.pallas_call(
        _proj_kernel, grid=grid, in_specs=in_specs, out_specs=out_specs, out_shape=out_shape,
        compiler_params=pltpu.CompilerParams(
            dimension_semantics=("parallel", "parallel"), vmem_limit_bytes=VMEM_LIMIT_BYTES),
        name="proj",
    )(x, pos, *consts)


SUM_ROWS = 16


class _AttnState:
    def __init__(self, qt_ref, s_ref, cmax_ref, m_ref, acc_ref):
        self.qt = qt_ref
        self.s = s_ref
        self.cmax = cmax_ref
        self.m = m_ref
        self.acc = acc_ref

    @staticmethod
    def scratch_shapes(dk, dv, tk, nq):
        return [pltpu.VMEM((dk, nq), BF16), pltpu.VMEM((2, tk, nq), F32), pltpu.VMEM((2, 1, nq), F32),
                pltpu.VMEM((1, nq), F32), pltpu.VMEM((dv + SUM_ROWS, nq), F32)]


def _causal_attention(st, k_ref, vt_ref, i, tk, maps):
    nq = st.qt.shape[1]
    dv = vt_ref.shape[1]
    group = maps * tk
    subtiles = nq // group
    assert subtiles % 2 == 0 or subtiles == 1
    n = i * subtiles
    ones = jnp.ones((SUM_ROWS, tk), BF16)
    visible = (lax.broadcasted_iota(jnp.int32, (tk, tk), 0) <= lax.broadcasted_iota(jnp.int32, (tk, tk), 1))

    def store_scores(slot, c0, s, boundary):
        if boundary:
            s = jnp.where(jnp.concatenate([visible] * (s.shape[1] // tk), axis=1), s, NEG_INF)
        st.s[slot, :, c0:c0 + s.shape[1]] = s
        st.cmax[slot, :, c0:c0 + s.shape[1]] = jnp.max(s, axis=0, keepdims=True)

    def scores(c, slot, c0=0, diagonal=False):
        k = k_ref[pl.ds(pl.multiple_of(c * tk, tk), tk), :]
        s = _dot(k, st.qt[:, c0:])
        if diagonal:
            store_scores(slot, c0, s[:, :group], True)
            if c0 + group < nq:
                store_scores(slot, c0 + group, s[:, group:], False)
        else:
            store_scores(slot, c0, s, False)

    def softmax_pv(c, slot, c0=0):
        m_prev = st.m[:, c0:]
        m_new = jnp.maximum(m_prev, st.cmax[slot, :, c0:])
        pt = jnp.exp2(st.s[slot, :, c0:] - m_new).astype(BF16)
        st.m[:, c0:] = m_new
        values = jnp.concatenate([vt_ref[c], ones], axis=0)
        st.acc[:, c0:] = jnp.exp2(m_prev - m_new) * st.acc[:, c0:] + _dot(values, pt)

    scores(0, 0)
    st.acc[...] = jnp.zeros_like(st.acc)
    st.m[...] = jnp.full_like(st.m, NEG_INF)

    def key_tile(t, carry):
        for u in range(subtiles):
            c = t * subtiles + u
            scores(c + 1, (u + 1) % 2)
            softmax_pv(c, u % 2)
        return carry

    lax.fori_loop(0, i, key_tile, 0)

    store_scores(0, 0, st.s[0, :, 0:group], True)
    for r in range(1, subtiles):
        scores(n + r, r % 2, r * group, diagonal=True)
        softmax_pv(n + r - 1, (r - 1) % 2, (r - 1) * group)
    softmax_pv(n + subtiles - 1, (subtiles - 1) % 2, (subtiles - 1) * group)
    return st.acc[0:dv, :] * (1.0 / st.acc[dv:dv + 1, :])


def _diff_attn_kernel(qt_ref, k_ref, vt_ref, lam_ref, g_ref, o_ref, *scratch):
    i = pl.program_id(2)
    subtiles, dk, tk = qt_ref.shape
    st = _AttnState(*scratch)
    feature = lax.broadcasted_iota(jnp.int32, (dk, 1), 0)
    for s in range(subtiles):
        q = qt_ref[s]
        zero = jnp.zeros_like(q)
        st.qt[:, 2 * s * tk:(2 * s + 1) * tk] = jnp.where(feature < DIFF_HEAD_DIM, q, zero)
        st.qt[:, (2 * s + 1) * tk:(2 * s + 2) * tk] = jnp.where(feature >= DIFF_HEAD_DIM, q, zero)
    ot = _causal_attention(st, k_ref, vt_ref, i, tk, 2)

    lq = lam_ref[...]
    lam = (jnp.exp(jnp.sum(lq[0:1] * lq[1:2], axis=-1, keepdims=True))
           - jnp.exp(jnp.sum(lq[2:3] * lq[3:4], axis=-1, keepdims=True)) + LAMBDA_INIT)
    for s in range(subtiles):
        c = 2 * s * tk
        od = ot[:, c:c + tk] - lam * ot[:, c + tk:c + 2 * tk]
        od = od * lax.rsqrt(jnp.mean(od * od, axis=0, keepdims=True) + RMS_EPS)
        o_ref[s * tk:(s + 1) * tk, :] = (od.T * (g_ref[...] * (1.0 - LAMBDA_INIT))).astype(BF16)


def _mla_attn_kernel(qt_ref, k_ref, vt_ref, o_ref, *scratch):
    i = pl.program_id(2)
    subtiles, _, tk = qt_ref.shape
    st = _AttnState(*scratch)
    for s in range(subtiles):
        st.qt[:, s * tk:(s + 1) * tk] = qt_ref[s]
    ot = _causal_attention(st, k_ref, vt_ref, i, tk, 1)
    for s in range(subtiles):
        o_ref[s * tk:(s + 1) * tk, :] = ot[:, s * tk:(s + 1) * tk].T.astype(BF16)


def _attn_call(body, name, qt, k, vt, extra, *, maps, columns):
    B, H, nc, dk, tk = qt.shape
    S = k.shape[2]
    dv = vt.shape[3]
    subtiles = min(columns // (maps * tk), nc)
    tq = subtiles * tk
    assert S % tq == 0
    return pl.pallas_call(
        body,
        grid=(B, H, S // tq),
        in_specs=[
            pl.BlockSpec((None, None, subtiles, dk, tk), lambda b, h, i: (b, h, i, 0, 0)),
            pl.BlockSpec((None, None, S, dk), lambda b, h, i: (b, h, 0, 0)),
            pl.BlockSpec((None, None, nc, dv, tk), lambda b, h, i: (b, h, 0, 0, 0)),
        ] + [pl.BlockSpec(e.shape, lambda b, h, i: (0, 0)) for e in extra],
        out_specs=pl.BlockSpec((None, None, tq, dv), lambda b, h, i: (b, h, i, 0)),
        out_shape=jax.ShapeDtypeStruct((B, H, S, dv), BF16),
        scratch_shapes=_AttnState.scratch_shapes(dk, dv, tk, maps * tq),
        compiler_params=pltpu.CompilerParams(
            dimension_semantics=("parallel", "parallel", "arbitrary"), vmem_limit_bytes=VMEM_LIMIT_BYTES),
        name=name,
    )(qt, k, vt, *extra)


def _out_kernel(x_ref, lng_ref, lnb_ref, oa_ref, za_ref, ob_ref, zb_ref, g_ref, p_ref,
                woa_ref, wob_ref, wout_ref, wpg_ref, bpg_ref, wpp_ref, pog_ref, pob_ref, y_ref):
    for r0 in range(0, x_ref.shape[0], MERGE_SUBTILE):
        rows = slice(r0, r0 + MERGE_SUBTILE)
        xn = _layer_norm(x_ref[rows, :], lng_ref[...], lnb_ref[...])
        oa = jnp.concatenate([oa_ref[h, rows, :] for h in range(DIFF_HEADS)], axis=-1)
        ob = jnp.concatenate([ob_ref[h, rows, :] for h in range(MLA_HEADS)], axis=-1)
        ya = _dot(oa * za_ref[rows, :], woa_ref[...])
        yb = _dot(ob * zb_ref[rows, :], wob_ref[...])
        g = g_ref[rows, :]
        merged = g[:, :D_MODEL].astype(F32) * ya + g[:, D_MODEL:].astype(F32) * yb
        y = DEEPNORM_ALPHA * xn + _dot(merged.astype(BF16), wout_ref[...])
        gate = jax.nn.sigmoid(_dot(y.astype(BF16), wpg_ref[...]) + bpg_ref[...])
        y = y + gate * _dot(p_ref[rows, :].astype(BF16), wpp_ref[...])
        y_ref[rows, :] = _layer_norm(y, pog_ref[...], pob_ref[...])


def _out_call(x, lng, lnb, oa, za, ob, zb, g, p, woa, wob, wout, wpg, bpg, wpp, pog, pob, *, tm):
    B, S, D = x.shape
    row = lambda b, i: (b, i, 0)
    head = lambda b, i: (b, 0, i, 0)
    in_specs = [
        pl.BlockSpec((None, tm, D), row), _resident(lng.shape), _resident(lnb.shape),
        pl.BlockSpec((None, DIFF_HEADS, tm, HEAD_LANES), head),
        pl.BlockSpec((None, tm, DIFF_WIDTH), row),
        pl.BlockSpec((None, MLA_HEADS, tm, MLA_V), head),
        pl.BlockSpec((None, tm, MLA_WIDTH), row),
        pl.BlockSpec((None, tm, 2 * D), row),
        pl.BlockSpec((None, tm, PLE_DIM), row),
        _resident(woa.shape), _resident(wob.shape), _resident(wout.shape), _resident(wpg.shape),
        _resident(bpg.shape), _resident(wpp.shape), _resident(pog.shape), _resident(pob.shape),
    ]
    return pl.pallas_call(
        _out_kernel, grid=(B, S // tm), in_specs=in_specs,
        out_specs=pl.BlockSpec((None, tm, D), row),
        out_shape=jax.ShapeDtypeStruct((B, S, D), F32),
        compiler_params=pltpu.CompilerParams(
            dimension_semantics=("parallel", "parallel"), vmem_limit_bytes=VMEM_LIMIT_BYTES),
        name="merge_out",
    )(x, lng, lnb, oa, za, ob, zb, g, p, woa, wob, wout, wpg, bpg, wpp, pog, pob)


WIN_LAYOUT_ROWS = 128

W_KA = DIFF_QK_WIDTH
W_VA = W_KA + DIFF_QK_WIDTH
W_ZA = W_VA + DIFF_WIDTH
W_KPE = W_ZA + DIFF_WIDTH + MLA_Q_LORA + MLA_KV_LORA
W_ZB = W_KPE + MLA_ROPE


def _win_layout_kernel(w_ref, win_ref, wqat_ref, wvat_ref):
    wqat_ref[...] = w_ref[:, 0:W_KA].T.astype(BF16)
    wvat_ref[...] = w_ref[:, W_VA:W_ZA].T.astype(BF16)
    win_ref[:, OFF_KA:OFF_ZA] = w_ref[:, W_KA:W_VA].astype(BF16)
    win_ref[:, OFF_ZA:OFF_KPE] = w_ref[:, W_ZA:W_KPE].astype(BF16)
    win_ref[:, OFF_KPE:OFF_KPE + MLA_ROPE] = w_ref[:, W_KPE:W_ZB].astype(BF16)
    win_ref[:, OFF_KPE + MLA_ROPE:OFF_ZB] = jnp.zeros((w_ref.shape[0], LANES - MLA_ROPE), BF16)
    win_ref[:, OFF_ZB:] = w_ref[:, W_ZB:].astype(BF16)


def _win_layout_call(w_in):
    _, D, n_in = w_in.shape
    rows = WIN_LAYOUT_ROWS
    n_pad = n_in - DIFF_QK_WIDTH - DIFF_WIDTH + LANES - MLA_ROPE
    return pl.pallas_call(
        _win_layout_kernel, grid=(D // rows,),
        in_specs=[pl.BlockSpec((None, rows, n_in), lambda i: (0, i, 0))],
        out_specs=[pl.BlockSpec((rows, n_pad), lambda i: (i, 0)),
                   pl.BlockSpec((DIFF_QK_WIDTH, rows), lambda i: (0, i)),
                   pl.BlockSpec((DIFF_WIDTH, rows), lambda i: (0, i))],
        out_shape=[jax.ShapeDtypeStruct((D, n_pad), BF16), jax.ShapeDtypeStruct((DIFF_QK_WIDTH, D), BF16),
                   jax.ShapeDtypeStruct((DIFF_WIDTH, D), BF16)],
        compiler_params=pltpu.CompilerParams(
            dimension_semantics=("parallel",), vmem_limit_bytes=VMEM_LIMIT_BYTES),
        name="win_layout",
    )(w_in)


def kernel(x, p, positions, ln_emb_g, ln_emb_b, w_in, b_gate, diff_lambda, diff_subln_g, w_o_a,
           mla_q_norm_g, mla_w_uq, mla_kv_norm_g, mla_w_ukv, w_o_b, w_out, ple_w_gate, ple_b_gate,
           ple_w_proj, ln_post_g, ln_post_b):
    B, S, D = x.shape
    assert D == D_MODEL and w_in.shape[0] == DEPTH == 1
    tk = min(KEY_CHUNK, S)
    tm = min(PROJ_ROWS, S)
    assert S % tm == 0
    row2 = lambda v: v.reshape(1, -1)

    win, wqat, wvat = _win_layout_call(w_in)
    wuq =mla_w_uq[0].astype(BF16).reshape(MLA_Q_LORA, MLA_HEADS, MLA_NOPE + MLA_ROPE)
    wuqt = jnp.concatenate([wuq[:, :, :MLA_NOPE].reshape(MLA_Q_LORA, -1),
                            wuq[:, :, MLA_NOPE:].reshape(MLA_Q_LORA, -1)], axis=1).T
    wukv = mla_w_ukv[0].astype(BF16).reshape(MLA_KV_LORA, MLA_HEADS, MLA_NOPE + MLA_V)
    wuk = wukv[:, :, :MLA_NOPE].reshape(MLA_KV_LORA, -1)
    wuvt = wukv[:, :, MLA_NOPE:].reshape(MLA_KV_LORA, -1).T
    inv = ROPE_THETA ** (-jnp.arange(0, DIFF_HEAD_DIM, 2, dtype=F32) / DIFF_HEAD_DIM)
    inv = jnp.tile(inv, LANES // inv.shape[0]).reshape(1, LANES)

    consts = (inv, row2(ln_emb_g), row2(ln_emb_b), win, wqat, wvat, row2(b_gate[0]),
              row2(mla_q_norm_g[0]), wuqt, row2(mla_kv_norm_g[0]), wuk, wuvt)
    qat, ka, vat, za, qbt, kb, vbt, zb, g = _proj_call(x, positions.reshape(B, S, 1), consts, tm=tm, tk=tk)
    oa = _attn_call(_diff_attn_kernel, "diff_attn", qat, ka, vat,
                    (diff_lambda[0], row2(diff_subln_g[0])), maps=2, columns=DIFF_ATTN_COLUMNS)
    ob = _attn_call(_mla_attn_kernel, "mla_attn", qbt, kb, vbt, (), maps=1, columns=MLA_ATTN_COLUMNS)
    return _out_call(
        x, row2(ln_emb_g), row2(ln_emb_b), oa, za, ob, zb, g, p[0],
        w_o_a[0].astype(BF16), w_o_b[0].astype(BF16), w_out[0].astype(BF16), ple_w_gate[0].astype(BF16),
        row2(ple_b_gate[0]), ple_w_proj[0].astype(BF16), row2(ln_post_g[0]), row2(ln_post_b[0]), tm=min(MERGE_ROWS, S))
```

```python
import functools
import math

import jax
import jax.numpy as jnp
from jax import lax
from jax.experimental import pallas as pl
from jax.experimental.pallas import tpu as pltpu

F32 = jnp.float32
BF16 = jnp.bfloat16

D_MODEL = 1024
DEPTH = 1
PLE_DIM = 256
ROPE_THETA = 10000.0
LN_EPS = 1e-5
RMS_EPS = 1e-6
NEG_INF = -1e30

DIFF_HEADS = 8
DIFF_HEAD_DIM = 64
DIFF_QK_WIDTH = 2 * DIFF_HEADS * DIFF_HEAD_DIM
DIFF_WIDTH = DIFF_HEADS * 2 * DIFF_HEAD_DIM

MLA_HEADS = 8
MLA_Q_LORA = 384
MLA_KV_LORA = 256
MLA_NOPE = 128
MLA_ROPE = 64
MLA_V = 128
MLA_WIDTH = MLA_HEADS * MLA_V

DEEPNORM_ALPHA = (2 * DEPTH) ** 0.25
LAMBDA_INIT = 0.8 - 0.6 * math.exp(-0.3 * 0)

LANES = 128
HEAD_LANES = 2 * DIFF_HEAD_DIM
MLA_QK_PAD = 256
LOG2E = math.log2(math.e)
HALF_ROT = DIFF_HEAD_DIM // 2

KEY_CHUNK = 256
DIFF_ATTN_COLUMNS = 4096
MLA_ATTN_COLUMNS = 2048
PROJ_ROWS = KEY_CHUNK
MERGE_SUBTILE = 256
MERGE_ROWS = 2 * MERGE_SUBTILE

OFF_KA = 0
OFF_ZA = OFF_KA + DIFF_QK_WIDTH
OFF_CQ = OFF_ZA + DIFF_WIDTH
OFF_CKV = OFF_CQ + MLA_Q_LORA
OFF_KPE = OFF_CKV + MLA_KV_LORA
OFF_ZB = OFF_KPE + LANES
OFF_G = OFF_ZB + MLA_WIDTH

VMEM_LIMIT_BYTES = 56 * 1024 * 1024


def _layer_norm(x, g, b):
    mu = jnp.mean(x, axis=-1, keepdims=True)
    xc = x - mu
    var = jnp.mean(xc * xc, axis=-1, keepdims=True)
    return xc * lax.rsqrt(var + LN_EPS) * g + b


def _rms_norm(x, g):
    return x * lax.rsqrt(jnp.mean(x * x, axis=-1, keepdims=True) + RMS_EPS) * g


def _dot(a, b):
    return jnp.dot(a, b, preferred_element_type=F32)


def _dot_nt(a, b):
    return lax.dot_general(a, b, (((1,), (1,)), ((), ())), preferred_element_type=F32)


def _proj_kernel(x_ref, pos_ref, inv_ref, lng_ref, lnb_ref, win_ref, wqat_ref, wvat_ref, bg_ref,
                 qng_ref, wuqt_ref, kvg_ref, wuk_ref, wuvt_ref,
                 qat_ref, ka_ref, vat_ref, za_ref, qbt_ref, kb_ref, vbt_ref, zb_ref, g_ref):
    tk = qat_ref.shape[-1]
    for r in range(x_ref.shape[0] // tk):
        _proj_rows(r, slice(r * tk, (r + 1) * tk), x_ref, pos_ref, inv_ref, lng_ref, lnb_ref, win_ref,
                   wqat_ref, wvat_ref, bg_ref, qng_ref, wuqt_ref, kvg_ref, wuk_ref, wuvt_ref,
                   qat_ref, ka_ref, vat_ref, za_ref, qbt_ref, kb_ref, vbt_ref, zb_ref, g_ref)


def _proj_rows(r, rows, x_ref, pos_ref, inv_ref, lng_ref, lnb_ref, win_ref, wqat_ref, wvat_ref, bg_ref,
               qng_ref, wuqt_ref, kvg_ref, wuk_ref, wuvt_ref,
               qat_ref, ka_ref, vat_ref, za_ref, qbt_ref, kb_ref, vbt_ref, zb_ref, g_ref):
    xn = _layer_norm(x_ref[rows, :], lng_ref[...], lnb_ref[...]).astype(BF16)

    lane = lax.broadcasted_iota(jnp.int32, (1, LANES), 1)
    group = lane // HALF_ROT
    ngroups = LANES // HALF_ROT
    quarter = (rows.stop - rows.start) // ngroups
    pos = pos_ref[rows, :].astype(F32)
    ang = pos[0:quarter] * inv_ref[...]
    for j in range(1, ngroups):
        ang = jnp.where(group == j, pos[j * quarter:(j + 1) * quarter] * inv_ref[...], ang)

    def replicate(compact):
        parts = []
        for j in range(ngroups):
            one = jnp.where(group == j, compact, 0.0)
            two = one + pltpu.roll(one, 2 * HALF_ROT, 1)
            parts.append(two + pltpu.roll(two, HALF_ROT, 1))
        return jnp.concatenate(parts, axis=0)

    cos = replicate(jnp.cos(ang))
    sin = replicate(jnp.sin(ang))
    first_half = (lane % DIFF_HEAD_DIM) < HALF_ROT
    sin_signed = jnp.where(first_half, -sin, sin)
    cos_t = cos.T
    sin_signed_t = sin_signed.T

    def rope(t):
        partner = jnp.where(first_half, pltpu.roll(t, LANES - HALF_ROT, 1), pltpu.roll(t, HALF_ROT, 1))
        return t * cos + partner * sin_signed

    def rope_t(t):
        h = HALF_ROT
        halves = [t[j * h:(j + 1) * h] for j in range(t.shape[0] // h)]
        partner = jnp.concatenate([halves[j ^ 1] for j in range(len(halves))], axis=0)
        return t * cos_t[:t.shape[0]] + partner * sin_signed_t[:t.shape[0]]

    def proj(off, width):
        return _dot(xn, win_ref[:, off:off + width])

    g_ref[rows, :] = jax.nn.sigmoid(proj(OFF_G, 2 * D_MODEL) + bg_ref[...]).astype(BF16)
    hzb = proj(OFF_ZB, MLA_WIDTH)
    zb_ref[rows, :] = (hzb * jax.nn.sigmoid(hzb)).astype(BF16)
    hz = proj(OFF_ZA, DIFF_WIDTH)
    za_ref[rows, :] = (hz * jax.nn.sigmoid(hz)).astype(BF16)
    hvt = _dot_nt(wvat_ref[...], xn)
    for h in range(DIFF_HEADS):
        vat_ref[h, r] = hvt[h * HEAD_LANES:(h + 1) * HEAD_LANES].astype(BF16)

    latent = proj(OFF_CQ, OFF_ZB - OFF_CQ)
    cq = _rms_norm(latent[:, :MLA_Q_LORA], qng_ref[...]).astype(BF16)
    ckv = _rms_norm(latent[:, MLA_Q_LORA:MLA_Q_LORA + MLA_KV_LORA], kvg_ref[...]).astype(BF16)
    uvt = _dot_nt(wuvt_ref[...], ckv)
    for h in range(MLA_HEADS):
        vbt_ref[h, r] = uvt[h * MLA_V:(h + 1) * MLA_V].astype(BF16)

    qa_scale = DIFF_HEAD_DIM ** -0.5 * LOG2E
    hqt = _dot_nt(wqat_ref[...], xn)
    for h in range(DIFF_HEADS):
        qat_ref[h, r] = (rope_t(hqt[h * HEAD_LANES:(h + 1) * HEAD_LANES]) * qa_scale).astype(BF16)
    hk = proj(OFF_KA, DIFF_QK_WIDTH)
    for h in range(DIFF_HEADS):
        ka_ref[h, rows, :] = rope(hk[:, h * HEAD_LANES:(h + 1) * HEAD_LANES]).astype(BF16)

    qb_scale = (MLA_NOPE + MLA_ROPE) ** -0.5 * LOG2E
    uqt = _dot_nt(wuqt_ref[...], cq)
    pe0 = MLA_HEADS * MLA_NOPE
    pe_end = MLA_NOPE + MLA_ROPE
    for h in range(MLA_HEADS):
        qbt_ref[h, r, 0:MLA_NOPE, :] = (uqt[h * MLA_NOPE:(h + 1) * MLA_NOPE] * qb_scale).astype(BF16)
        qbt_ref[h, r, MLA_NOPE:pe_end, :] = (
            rope_t(uqt[pe0 + h * MLA_ROPE:pe0 + (h + 1) * MLA_ROPE]) * qb_scale).astype(BF16)
        qbt_ref[h, r, pe_end:MLA_QK_PAD, :] = jnp.zeros((MLA_QK_PAD - pe_end, uqt.shape[1]), BF16)

    k_rot = rope(latent[:, MLA_Q_LORA + MLA_KV_LORA:]).astype(BF16)
    uk = _dot(ckv, wuk_ref[...])
    for h in range(MLA_HEADS):
        kb_ref[h, rows, 0:LANES] = uk[:, h * MLA_NOPE:(h + 1) * MLA_NOPE].astype(BF16)
        kb_ref[h, rows, LANES:2 * LANES] = k_rot


def _resident(shape):
    nd = len(shape)
    return pl.BlockSpec(shape, lambda *_: (0,) * nd, pipeline_mode=pl.Buffered(1))


def _proj_call(x, pos, consts, *, tm, tk):
    B, S, D = x.shape
    grid = (B, S // tm)
    row = lambda b, i: (b, i, 0)
    head = lambda b, i: (b, 0, i, 0)
    head_t = lambda b, i: (b, 0, i, 0, 0)
    bf = lambda shape: jax.ShapeDtypeStruct(shape, BF16)
    nc = S // tk
    out_shape = [
        bf((B, DIFF_HEADS, nc, HEAD_LANES, tk)), bf((B, DIFF_HEADS, S, HEAD_LANES)),
        bf((B, DIFF_HEADS, nc, HEAD_LANES, tk)), bf((B, S, DIFF_WIDTH)),
        bf((B, MLA_HEADS, nc, MLA_QK_PAD, tk)), bf((B, MLA_HEADS, S, MLA_QK_PAD)),
        bf((B, MLA_HEADS, nc, MLA_V, tk)), bf((B, S, MLA_WIDTH)), bf((B, S, 2 * D)),
    ]
    out_specs = [
        pl.BlockSpec((None, DIFF_HEADS, tm // tk, HEAD_LANES, tk), head_t),
        pl.BlockSpec((None, DIFF_HEADS, tm, HEAD_LANES), head),
        pl.BlockSpec((None, DIFF_HEADS, tm // tk, HEAD_LANES, tk), head_t),
        pl.BlockSpec((None, tm, DIFF_WIDTH), row),
        pl.BlockSpec((None, MLA_HEADS, tm // tk, MLA_QK_PAD, tk), head_t),
        pl.BlockSpec((None, MLA_HEADS, tm, MLA_QK_PAD), head),
        pl.BlockSpec((None, MLA_HEADS, tm // tk, MLA_V, tk), head_t),
        pl.BlockSpec((None, tm, MLA_WIDTH), row),
        pl.BlockSpec((None, tm, 2 * D), row),
    ]
    in_specs = [pl.BlockSpec((None, tm, D), row), pl.BlockSpec((None, tm, 1), row)]
    in_specs += [_resident(c.shape) for c in consts]
    return pl.pallas_call(
        _proj_kernel, grid=grid, in_specs=in_specs, out_specs=out_specs, out_shape=out_shape,
        compiler_params=pltpu.CompilerParams(
            dimension_semantics=("parallel", "parallel"), vmem_limit_bytes=VMEM_LIMIT_BYTES),
        name="proj",
    )(x, pos, *consts)


SUM_ROWS = 16


class _AttnState:
    def __init__(self, qt_ref, s_ref, cmax_ref, m_ref, acc_ref):
        self.qt = qt_ref
        self.s = s_ref
        self.cmax = cmax_ref
        self.m = m_ref
        self.acc = acc_ref

    @staticmethod
    def scratch_shapes(dk, dv, tk, nq):
        return [pltpu.VMEM((dk, nq), BF16), pltpu.VMEM((2, tk, nq), F32), pltpu.VMEM((2, 1, nq), F32),
                pltpu.VMEM((1, nq), F32), pltpu.VMEM((dv + SUM_ROWS, nq), F32)]


def _causal_attention(st, k_ref, vt_ref, i, tk, maps):
    nq = st.qt.shape[1]
    dv = vt_ref.shape[1]
    group = maps * tk
    subtiles = nq // group
    assert subtiles % 2 == 0 or subtiles == 1
    n = i * subtiles
    ones = jnp.ones((SUM_ROWS, tk), BF16)
    visible = (lax.broadcasted_iota(jnp.int32, (tk, tk), 0) <= lax.broadcasted_iota(jnp.int32, (tk, tk), 1))

    def store_scores(slot, c0, s, boundary):
        if boundary:
            s = jnp.where(jnp.concatenate([visible] * (s.shape[1] // tk), axis=1), s, NEG_INF)
        st.s[slot, :, c0:c0 + s.shape[1]] = s
        st.cmax[slot, :, c0:c0 + s.shape[1]] = jnp.max(s, axis=0, keepdims=True)

    def scores(c, slot, c0=0, diagonal=False):
        k = k_ref[pl.ds(pl.multiple_of(c * tk, tk), tk), :]
        s = _dot(k, st.qt[:, c0:])
        if diagonal:
            store_scores(slot, c0, s[:, :group], True)
            if c0 + group < nq:
                store_scores(slot, c0 + group, s[:, group:], False)
        else:
            store_scores(slot, c0, s, False)

    def softmax_pv(c, slot, c0=0):
        m_prev = st.m[:, c0:]
        m_new = jnp.maximum(m_prev, st.cmax[slot, :, c0:])
        pt = jnp.exp2(st.s[slot, :, c0:] - m_new).astype(BF16)
        st.m[:, c0:] = m_new
        values = jnp.concatenate([vt_ref[c], ones], axis=0)
        st.acc[:, c0:] = jnp.exp2(m_prev - m_new) * st.acc[:, c0:] + _dot(values, pt)

    scores(0, 0)
    st.acc[...] = jnp.zeros_like(st.acc)
    st.m[...] = jnp.full_like(st.m, NEG_INF)

    def key_tile(t, carry):
        for u in range(subtiles):
            c = t * subtiles + u
            scores(c + 1, (u + 1) % 2)
            softmax_pv(c, u % 2)
        return carry

    lax.fori_loop(0, i, key_tile, 0)

    store_scores(0, 0, st.s[0, :, 0:group], True)
    for r in range(1, subtiles):
        scores(n + r, r % 2, r * group, diagonal=True)
        softmax_pv(n + r - 1, (r - 1) % 2, (r - 1) * group)
    softmax_pv(n + subtiles - 1, (subtiles - 1) % 2, (subtiles - 1) * group)
    return st.acc[0:dv, :] * (1.0 / st.acc[dv:dv + 1, :])


def _diff_attn_kernel(qt_ref, k_ref, vt_ref, lam_ref, g_ref, o_ref, *scratch):
    i = pl.program_id(2)
    subtiles, dk, tk = qt_ref.shape
    st = _AttnState(*scratch)
    feature = lax.broadcasted_iota(jnp.int32, (dk, 1), 0)
    for s in range(subtiles):
        q = qt_ref[s]
        zero = jnp.zeros_like(q)
        st.qt[:, 2 * s * tk:(2 * s + 1) * tk] = jnp.where(feature < DIFF_HEAD_DIM, q, zero)
        st.qt[:, (2 * s + 1) * tk:(2 * s + 2) * tk] = jnp.where(feature >= DIFF_HEAD_DIM, q, zero)
    ot = _causal_attention(st, k_ref, vt_ref, i, tk, 2)

    lq = lam_ref[...]
    lam = (jnp.exp(jnp.sum(lq[0:1] * lq[1:2], axis=-1, keepdims=True))
           - jnp.exp(jnp.sum(lq[2:3] * lq[3:4], axis=-1, keepdims=True)) + LAMBDA_INIT)
    for s in range(subtiles):
        c = 2 * s * tk
        od = ot[:, c:c + tk] - lam * ot[:, c + tk:c + 2 * tk]
        od = od * lax.rsqrt(jnp.mean(od * od, axis=0, keepdims=True) + RMS_EPS)
        o_ref[s * tk:(s + 1) * tk, :] = (od.T * (g_ref[...] * (1.0 - LAMBDA_INIT))).astype(BF16)


def _mla_attn_kernel(qt_ref, k_ref, vt_ref, o_ref, *scratch):
    i = pl.program_id(2)
    subtiles, _, tk = qt_ref.shape
    st = _AttnState(*scratch)
    for s in range(subtiles):
        st.qt[:, s * tk:(s + 1) * tk] = qt_ref[s]
    ot = _causal_attention(st, k_ref, vt_ref, i, tk, 1)
    for s in range(subtiles):
        o_ref[s * tk:(s + 1) * tk, :] = ot[:, s * tk:(s + 1) * tk].T.astype(BF16)


def _attn_call(body, name, qt, k, vt, extra, *, maps, columns):
    B, H, nc, dk, tk = qt.shape
    S = k.shape[2]
    dv = vt.shape[3]
    subtiles = min(columns // (maps * tk), nc)
    tq = subtiles * tk
    assert S % tq == 0
    return pl.pallas_call(
        body,
        grid=(B, H, S // tq),
        in_specs=[
            pl.BlockSpec((None, None, subtiles, dk, tk), lambda b, h, i: (b, h, i, 0, 0)),
            pl.BlockSpec((None, None, S, dk), lambda b, h, i: (b, h, 0, 0)),
            pl.BlockSpec((None, None, nc, dv, tk), lambda b, h, i: (b, h, 0, 0, 0)),
        ] + [pl.BlockSpec(e.shape, lambda b, h, i: (0, 0)) for e in extra],
        out_specs=pl.BlockSpec((None, None, tq, dv), lambda b, h, i: (b, h, i, 0)),
        out_shape=jax.ShapeDtypeStruct((B, H, S, dv), BF16),
        scratch_shapes=_AttnState.scratch_shapes(dk, dv, tk, maps * tq),
        compiler_params=pltpu.CompilerParams(
            dimension_semantics=("parallel", "parallel", "arbitrary"), vmem_limit_bytes=VMEM_LIMIT_BYTES),
        name=name,
    )(qt, k, vt, *extra)


def _out_kernel(x_ref, lng_ref, lnb_ref, oa_ref, za_ref, ob_ref, zb_ref, g_ref, p_ref,
                woa_ref, wob_ref, wout_ref, wpg_ref, bpg_ref, wpp_ref, pog_ref, pob_ref, y_ref):
    for r0 in range(0, x_ref.shape[0], MERGE_SUBTILE):
        rows = slice(r0, r0 + MERGE_SUBTILE)
        xn = _layer_norm(x_ref[rows, :], lng_ref[...], lnb_ref[...])
        oa = jnp.concatenate([oa_ref[h, rows, :] for h in range(DIFF_HEADS)], axis=-1)
        ob = jnp.concatenate([ob_ref[h, rows, :] for h in range(MLA_HEADS)], axis=-1)
        ya = _dot(oa * za_ref[rows, :], woa_ref[...])
        yb = _dot(ob * zb_ref[rows, :], wob_ref[...])
        g = g_ref[rows, :]
        merged = g[:, :D_MODEL].astype(F32) * ya + g[:, D_MODEL:].astype(F32) * yb
        y = DEEPNORM_ALPHA * xn + _dot(merged.astype(BF16), wout_ref[...])
        gate = jax.nn.sigmoid(_dot(y.astype(BF16), wpg_ref[...]) + bpg_ref[...])
        y = y + gate * _dot(p_ref[rows, :].astype(BF16), wpp_ref[...])
        y_ref[rows, :] = _layer_norm(y, pog_ref[...], pob_ref[...])


def _out_call(x, lng, lnb, oa, za, ob, zb, g, p, woa, wob, wout, wpg, bpg, wpp, pog, pob, *, tm):
    B, S, D = x.shape
    row = lambda b, i: (b, i, 0)
    head = lambda b, i: (b, 0, i, 0)
    in_specs = [
        pl.BlockSpec((None, tm, D), row), _resident(lng.shape), _resident(lnb.shape),
        pl.BlockSpec((None, DIFF_HEADS, tm, HEAD_LANES), head),
        pl.BlockSpec((None, tm, DIFF_WIDTH), row),
        pl.BlockSpec((None, MLA_HEADS, tm, MLA_V), head),
        pl.BlockSpec((None, tm, MLA_WIDTH), row),
        pl.BlockSpec((None, tm, 2 * D), row),
        pl.BlockSpec((None, tm, PLE_DIM), row),
        _resident(woa.shape), _resident(wob.shape), _resident(wout.shape), _resident(wpg.shape),
        _resident(bpg.shape), _resident(wpp.shape), _resident(pog.shape), _resident(pob.shape),
    ]
    return pl.pallas_call(
        _out_kernel, grid=(B, S // tm), in_specs=in_specs,
        out_specs=pl.BlockSpec((None, tm, D), row),
        out_shape=jax.ShapeDtypeStruct((B, S, D), F32),
        compiler_params=pltpu.CompilerParams(
            dimension_semantics=("parallel", "parallel"), vmem_limit_bytes=VMEM_LIMIT_BYTES),
        name="merge_out",
    )(x, lng, lnb, oa, za, ob, zb, g, p, woa, wob, wout, wpg, bpg, wpp, pog, pob)


WIN_LAYOUT_ROWS = 128

W_KA = DIFF_QK_WIDTH
W_VA = W_KA + DIFF_QK_WIDTH
W_ZA = W_VA + DIFF_WIDTH
W_KPE = W_ZA + DIFF_WIDTH + MLA_Q_LORA + MLA_KV_LORA
W_ZB = W_KPE + MLA_ROPE


def _win_layout_kernel(wt_ref, win_ref, wqat_ref, wvat_ref):
    def put(off, lo, hi):
        win_ref[:, off:off + hi - lo] = wt_ref[lo:hi, :].T.astype(BF16)

    wqat_ref[...] = wt_ref[0:W_KA, :].astype(BF16)
    wvat_ref[...] = wt_ref[W_VA:W_ZA, :].astype(BF16)
    put(OFF_KA, W_KA, W_VA)
    put(OFF_ZA, W_ZA, W_KPE)
    put(OFF_KPE, W_KPE, W_ZB)
    win_ref[:, OFF_KPE + MLA_ROPE:OFF_ZB] = jnp.zeros((win_ref.shape[0], LANES - MLA_ROPE), BF16)
    put(OFF_ZB, W_ZB, wt_ref.shape[0])


def _win_layout_call(w_t):
    n_in, D = w_t.shape
    rows = WIN_LAYOUT_ROWS
    n_pad = n_in - DIFF_QK_WIDTH - DIFF_WIDTH + LANES - MLA_ROPE
    return pl.pallas_call(
        _win_layout_kernel, grid=(D // rows,),
        in_specs=[pl.BlockSpec((n_in, rows), lambda i: (0, i))],
        out_specs=[pl.BlockSpec((rows, n_pad), lambda i: (i, 0)),
                   pl.BlockSpec((DIFF_QK_WIDTH, rows), lambda i: (0, i)),
                   pl.BlockSpec((DIFF_WIDTH, rows), lambda i: (0, i))],
        out_shape=[jax.ShapeDtypeStruct((D, n_pad), BF16), jax.ShapeDtypeStruct((DIFF_QK_WIDTH, D), BF16),
                   jax.ShapeDtypeStruct((DIFF_WIDTH, D), BF16)],
        compiler_params=pltpu.CompilerParams(
            dimension_semantics=("parallel",), vmem_limit_bytes=VMEM_LIMIT_BYTES),
        name="win_layout",
    )(w_t)


def kernel(x, p, positions, ln_emb_g, ln_emb_b, w_in, b_gate, diff_lambda, diff_subln_g, w_o_a,
           mla_q_norm_g, mla_w_uq, mla_kv_norm_g, mla_w_ukv, w_o_b, w_out, ple_w_gate, ple_b_gate,
           ple_w_proj, ln_post_g, ln_post_b):
    B, S, D = x.shape
    assert D == D_MODEL and w_in.shape[0] == DEPTH == 1
    tk = min(KEY_CHUNK, S)
    tm = min(PROJ_ROWS, S)
    assert S % tm == 0
    row2 = lambda v: v.reshape(1, -1)

    win, wqat, wvat = _win_layout_call(w_in[0].T)
    wuq = mla_w_uq[0].astype(BF16).reshape(MLA_Q_LORA, MLA_HEADS, MLA_NOPE + MLA_ROPE)
    wuqt = jnp.concatenate([wuq[:, :, :MLA_NOPE].reshape(MLA_Q_LORA, -1),
                            wuq[:, :, MLA_NOPE:].reshape(MLA_Q_LORA, -1)], axis=1).T
    wukv = mla_w_ukv[0].astype(BF16).reshape(MLA_KV_LORA, MLA_HEADS, MLA_NOPE + MLA_V)
    wuk = wukv[:, :, :MLA_NOPE].reshape(MLA_KV_LORA, -1)
    wuvt = wukv[:, :, MLA_NOPE:].reshape(MLA_KV_LORA, -1).T
    inv = ROPE_THETA ** (-jnp.arange(0, DIFF_HEAD_DIM, 2, dtype=F32) / DIFF_HEAD_DIM)
    inv = jnp.tile(inv, LANES // inv.shape[0]).reshape(1, LANES)

    consts = (inv, row2(ln_emb_g), row2(ln_emb_b), win, wqat, wvat, row2(b_gate[0]),
              row2(mla_q_norm_g[0]), wuqt, row2(mla_kv_norm_g[0]), wuk, wuvt)
    qat, ka, vat, za, qbt, kb, vbt, zb, g = _proj_call(x, positions.reshape(B, S, 1), consts, tm=tm, tk=tk)
    oa = _attn_call(_diff_attn_kernel, "diff_attn", qat, ka, vat,
                    (diff_lambda[0], row2(diff_subln_g[0])), maps=2, columns=DIFF_ATTN_COLUMNS)
    ob = _attn_call(_mla_attn_kernel, "mla_attn", qbt, kb, vbt, (), maps=1, columns=MLA_ATTN_COLUMNS)
    return _out_call(
        x, row2(ln_emb_g), row2(ln_emb_b), oa, za, ob, zb, g, p[0],
        w_o_a[0].astype(BF16), w_o_b[0].astype(BF16), w_out[0].astype(BF16), ple_w_gate[0].astype(BF16),
        row2(ple_b_gate[0]), ple_w_proj[0].astype(BF16), row2(ln_post_g[0]), row2(ln_post_b[0]), tm=min(MERGE_ROWS, S))
```

```python
import functools
import math

import jax
import jax.numpy as jnp
from jax import lax
from jax.experimental import pallas as pl
from jax.experimental.pallas import tpu as pltpu

F32 = jnp.float32
BF16 = jnp.bfloat16

D_MODEL = 1024
DEPTH = 1
PLE_DIM = 256
ROPE_THETA = 10000.0
LN_EPS = 1e-5
RMS_EPS = 1e-6
NEG_INF = -1e30

DIFF_HEADS = 8
DIFF_HEAD_DIM = 64
DIFF_QK_WIDTH = 2 * DIFF_HEADS * DIFF_HEAD_DIM
DIFF_WIDTH = DIFF_HEADS * 2 * DIFF_HEAD_DIM

MLA_HEADS = 8
MLA_Q_LORA = 384
MLA_KV_LORA = 256
MLA_NOPE = 128
MLA_ROPE = 64
MLA_V = 128
MLA_WIDTH = MLA_HEADS * MLA_V

DEEPNORM_ALPHA = (2 * DEPTH) ** 0.25
LAMBDA_INIT = 0.8 - 0.6 * math.exp(-0.3 * 0)

LANES = 128
HEAD_LANES = 2 * DIFF_HEAD_DIM
MLA_QK_PAD = 256
LOG2E = math.log2(math.e)
HALF_ROT = DIFF_HEAD_DIM // 2

KEY_CHUNK = 256
DIFF_ATTN_COLUMNS = 4096
MLA_ATTN_COLUMNS = 2048
ATTN_COLUMN_BLOCK = 512
PROJ_ROWS = KEY_CHUNK
MERGE_SUBTILE = 256
MERGE_ROWS = 2 * MERGE_SUBTILE

OFF_KA = 0
OFF_ZA = OFF_KA + DIFF_QK_WIDTH
OFF_CQ = OFF_ZA + DIFF_WIDTH
OFF_CKV = OFF_CQ + MLA_Q_LORA
OFF_KPE = OFF_CKV + MLA_KV_LORA
OFF_ZB = OFF_KPE + LANES
OFF_G = OFF_ZB + MLA_WIDTH

VMEM_LIMIT_BYTES = 56 * 1024 * 1024


def _layer_norm(x, g, b):
    mu = jnp.mean(x, axis=-1, keepdims=True)
    xc = x - mu
    var = jnp.mean(xc * xc, axis=-1, keepdims=True)
    return xc * lax.rsqrt(var + LN_EPS) * g + b


def _rms_norm(x, g):
    return x * lax.rsqrt(jnp.mean(x * x, axis=-1, keepdims=True) + RMS_EPS) * g


def _dot(a, b):
    return jnp.dot(a, b, preferred_element_type=F32)


def _dot_nt(a, b):
    return lax.dot_general(a, b, (((1,), (1,)), ((), ())), preferred_element_type=F32)


def _proj_kernel(x_ref, pos_ref, inv_ref, lng_ref, lnb_ref, win_ref, wqat_ref, wvat_ref, bg_ref,
                 qng_ref, wuqt_ref, kvg_ref, wuk_ref, wuvt_ref,
                 qat_ref, ka_ref, vat_ref, za_ref, qbt_ref, kb_ref, vbt_ref, zb_ref, g_ref):
    tk = qat_ref.shape[-1]
    for r in range(x_ref.shape[0] // tk):
        _proj_rows(r, slice(r * tk, (r + 1) * tk), x_ref, pos_ref, inv_ref, lng_ref, lnb_ref, win_ref,
                   wqat_ref, wvat_ref, bg_ref, qng_ref, wuqt_ref, kvg_ref, wuk_ref, wuvt_ref,
                   qat_ref, ka_ref, vat_ref, za_ref, qbt_ref, kb_ref, vbt_ref, zb_ref, g_ref)


def _proj_rows(r, rows, x_ref, pos_ref, inv_ref, lng_ref, lnb_ref, win_ref, wqat_ref, wvat_ref, bg_ref,
               qng_ref, wuqt_ref, kvg_ref, wuk_ref, wuvt_ref,
               qat_ref, ka_ref, vat_ref, za_ref, qbt_ref, kb_ref, vbt_ref, zb_ref, g_ref):
    xn = _layer_norm(x_ref[rows, :], lng_ref[...], lnb_ref[...]).astype(BF16)

    lane = lax.broadcasted_iota(jnp.int32, (1, LANES), 1)
    group = lane // HALF_ROT
    ngroups = LANES // HALF_ROT
    quarter = (rows.stop - rows.start) // ngroups
    pos = pos_ref[rows, :].astype(F32)
    ang = pos[0:quarter] * inv_ref[...]
    for j in range(1, ngroups):
        ang = jnp.where(group == j, pos[j * quarter:(j + 1) * quarter] * inv_ref[...], ang)

    def replicate(compact):
        parts = []
        for j in range(ngroups):
            one = jnp.where(group == j, compact, 0.0)
            two = one + pltpu.roll(one, 2 * HALF_ROT, 1)
            parts.append(two + pltpu.roll(two, HALF_ROT, 1))
        return jnp.concatenate(parts, axis=0)

    cos = replicate(jnp.cos(ang))
    sin = replicate(jnp.sin(ang))
    first_half = (lane % DIFF_HEAD_DIM) < HALF_ROT
    sin_signed = jnp.where(first_half, -sin, sin)
    cos_t = cos.T
    sin_signed_t = sin_signed.T

    def rope(t):
        partner = jnp.where(first_half, pltpu.roll(t, LANES - HALF_ROT, 1), pltpu.roll(t, HALF_ROT, 1))
        return t * cos + partner * sin_signed

    def rope_t(t):
        h = HALF_ROT
        halves = [t[j * h:(j + 1) * h] for j in range(t.shape[0] // h)]
        partner = jnp.concatenate([halves[j ^ 1] for j in range(len(halves))], axis=0)
        return t * cos_t[:t.shape[0]] + partner * sin_signed_t[:t.shape[0]]

    def proj(off, width):
        return _dot(xn, win_ref[:, off:off + width])

    g_ref[rows, :] = jax.nn.sigmoid(proj(OFF_G, 2 * D_MODEL) + bg_ref[...]).astype(BF16)
    hzb = proj(OFF_ZB, MLA_WIDTH)
    zb_ref[rows, :] = (hzb * jax.nn.sigmoid(hzb)).astype(BF16)
    hz = proj(OFF_ZA, DIFF_WIDTH)
    za_ref[rows, :] = (hz * jax.nn.sigmoid(hz)).astype(BF16)
    hvt = _dot_nt(wvat_ref[...], xn)
    for h in range(DIFF_HEADS):
        vat_ref[h, r] = hvt[h * HEAD_LANES:(h + 1) * HEAD_LANES].astype(BF16)

    latent = proj(OFF_CQ, OFF_ZB - OFF_CQ)
    cq = _rms_norm(latent[:, :MLA_Q_LORA], qng_ref[...]).astype(BF16)
    ckv = _rms_norm(latent[:, MLA_Q_LORA:MLA_Q_LORA + MLA_KV_LORA], kvg_ref[...]).astype(BF16)
    uvt = _dot_nt(wuvt_ref[...], ckv)
    for h in range(MLA_HEADS):
        vbt_ref[h, r] = uvt[h * MLA_V:(h + 1) * MLA_V].astype(BF16)

    qa_scale = DIFF_HEAD_DIM ** -0.5 * LOG2E
    hqt = _dot_nt(wqat_ref[...], xn)
    for h in range(DIFF_HEADS):
        qat_ref[h, r] = (rope_t(hqt[h * HEAD_LANES:(h + 1) * HEAD_LANES]) * qa_scale).astype(BF16)
    hk = proj(OFF_KA, DIFF_QK_WIDTH)
    for h in range(DIFF_HEADS):
        ka_ref[h, rows, :] = rope(hk[:, h * HEAD_LANES:(h + 1) * HEAD_LANES]).astype(BF16)

    qb_scale = (MLA_NOPE + MLA_ROPE) ** -0.5 * LOG2E
    uqt = _dot_nt(wuqt_ref[...], cq)
    pe0 = MLA_HEADS * MLA_NOPE
    pe_end = MLA_NOPE + MLA_ROPE
    for h in range(MLA_HEADS):
        qbt_ref[h, r, 0:MLA_NOPE, :] = (uqt[h * MLA_NOPE:(h + 1) * MLA_NOPE] * qb_scale).astype(BF16)
        qbt_ref[h, r, MLA_NOPE:pe_end, :] = (
            rope_t(uqt[pe0 + h * MLA_ROPE:pe0 + (h + 1) * MLA_ROPE]) * qb_scale).astype(BF16)
        qbt_ref[h, r, pe_end:MLA_QK_PAD, :] = jnp.zeros((MLA_QK_PAD - pe_end, uqt.shape[1]), BF16)

    k_rot = rope(latent[:, MLA_Q_LORA + MLA_KV_LORA:]).astype(BF16)
    uk = _dot(ckv, wuk_ref[...])
    for h in range(MLA_HEADS):
        kb_ref[h, rows, 0:LANES] = uk[:, h * MLA_NOPE:(h + 1) * MLA_NOPE].astype(BF16)
        kb_ref[h, rows, LANES:2 * LANES] = k_rot


def _resident(shape):
    nd = len(shape)
    return pl.BlockSpec(shape, lambda *_: (0,) * nd, pipeline_mode=pl.Buffered(1))


def _proj_call(x, pos, consts, *, tm, tk):
    B, S, D = x.shape
    grid = (B, S // tm)
    row = lambda b, i: (b, i, 0)
    head = lambda b, i: (b, 0, i, 0)
    head_t = lambda b, i: (b, 0, i, 0, 0)
    bf = lambda shape: jax.ShapeDtypeStruct(shape, BF16)
    nc = S // tk
    out_shape = [
        bf((B, DIFF_HEADS, nc, HEAD_LANES, tk)), bf((B, DIFF_HEADS, S, HEAD_LANES)),
        bf((B, DIFF_HEADS, nc, HEAD_LANES, tk)), bf((B, S, DIFF_WIDTH)),
        bf((B, MLA_HEADS, nc, MLA_QK_PAD, tk)), bf((B, MLA_HEADS, S, MLA_QK_PAD)),
        bf((B, MLA_HEADS, nc, MLA_V, tk)), bf((B, S, MLA_WIDTH)), bf((B, S, 2 * D)),
    ]
    out_specs = [
        pl.BlockSpec((None, DIFF_HEADS, tm // tk, HEAD_LANES, tk), head_t),
        pl.BlockSpec((None, DIFF_HEADS, tm, HEAD_LANES), head),
        pl.BlockSpec((None, DIFF_HEADS, tm // tk, HEAD_LANES, tk), head_t),
        pl.BlockSpec((None, tm, DIFF_WIDTH), row),
        pl.BlockSpec((None, MLA_HEADS, tm // tk, MLA_QK_PAD, tk), head_t),
        pl.BlockSpec((None, MLA_HEADS, tm, MLA_QK_PAD), head),
        pl.BlockSpec((None, MLA_HEADS, tm // tk, MLA_V, tk), head_t),
        pl.BlockSpec((None, tm, MLA_WIDTH), row),
        pl.BlockSpec((None, tm, 2 * D), row),
    ]
    in_specs = [pl.BlockSpec((None, tm, D), row), pl.BlockSpec((None, tm, 1), row)]
    in_specs += [_resident(c.shape) for c in consts]
    return pl.pallas_call(
        _proj_kernel, grid=grid, in_specs=in_specs, out_specs=out_specs, out_shape=out_shape,
        compiler_params=pltpu.CompilerParams(
            dimension_semantics=("parallel", "parallel"), vmem_limit_bytes=VMEM_LIMIT_BYTES),
        name="proj",
    )(x, pos, *consts)


SUM_ROWS = 16


class _AttnState:
    def __init__(self, qt_ref, s_ref, cmax_ref, m_ref, acc_ref):
        self.qt = qt_ref
        self.s = s_ref
        self.cmax = cmax_ref
        self.m = m_ref
        self.acc = acc_ref

    @staticmethod
    def scratch_shapes(dk, dv, tk, nq):
        return [pltpu.VMEM((dk, nq), BF16), pltpu.VMEM((2, tk, nq), F32), pltpu.VMEM((2, 1, nq), F32),
                pltpu.VMEM((1, nq), F32), pltpu.VMEM((dv + SUM_ROWS, nq), F32)]


def _causal_attention(st, k_ref, vt_ref, i, tk, maps):
    nq = st.qt.shape[1]
    dv = vt_ref.shape[1]
    group = maps * tk
    subtiles = nq // group
    assert subtiles % 2 == 0 or subtiles == 1
    n = i * subtiles
    ones = jnp.ones((SUM_ROWS, tk), BF16)
    visible = (lax.broadcasted_iota(jnp.int32, (tk, tk), 0) <= lax.broadcasted_iota(jnp.int32, (tk, tk), 1))

    def store_scores(slot, c0, s, boundary):
        if boundary:
            s = jnp.where(jnp.concatenate([visible] * (s.shape[1] // tk), axis=1), s, NEG_INF)
        st.s[slot, :, c0:c0 + s.shape[1]] = s
        st.cmax[slot, :, c0:c0 + s.shape[1]] = jnp.max(s, axis=0, keepdims=True)

    def column_blocks(c0):
        return [(a, min(a + ATTN_COLUMN_BLOCK, nq)) for a in range(c0, nq, ATTN_COLUMN_BLOCK)]

    def scores(c, slot, c0=0, diagonal=False):
        k = k_ref[pl.ds(pl.multiple_of(c * tk, tk), tk), :]
        for a, b in column_blocks(c0):
            s = _dot(k, st.qt[:, a:b])
            edge = max(0, min(b, c0 + group) - a) if diagonal else 0
            if edge > 0:
                store_scores(slot, a, s[:, :edge], True)
            if edge < b - a:
                store_scores(slot, a + edge, s[:, edge:], False)

    def softmax_pv(c, slot, c0=0):
        values = jnp.concatenate([vt_ref[c], ones], axis=0)
        for a, b in column_blocks(c0):
            m_prev = st.m[:, a:b]
            m_new = jnp.maximum(m_prev, st.cmax[slot, :, a:b])
            pt = jnp.exp2(st.s[slot, :, a:b] - m_new).astype(BF16)
            st.m[:, a:b] = m_new
            st.acc[:, a:b] = jnp.exp2(m_prev - m_new) * st.acc[:, a:b] + _dot(values, pt)

    scores(0, 0)
    st.acc[...] = jnp.zeros_like(st.acc)
    st.m[...] = jnp.full_like(st.m, NEG_INF)

    def key_tile(t, carry):
        for u in range(subtiles):
            c = t * subtiles + u
            scores(c + 1, (u + 1) % 2)
            softmax_pv(c, u % 2)
        return carry

    lax.fori_loop(0, i, key_tile, 0)

    store_scores(0, 0, st.s[0, :, 0:group], True)
    for r in range(1, subtiles):
        scores(n + r, r % 2, r * group, diagonal=True)
        softmax_pv(n + r - 1, (r - 1) % 2, (r - 1) * group)
    softmax_pv(n + subtiles - 1, (subtiles - 1) % 2, (subtiles - 1) * group)
    return st.acc[0:dv, :] * (1.0 / st.acc[dv:dv + 1, :])


def _diff_attn_kernel(qt_ref, k_ref, vt_ref, lam_ref, g_ref, o_ref, *scratch):
    i = pl.program_id(2)
    subtiles, dk, tk = qt_ref.shape
    st = _AttnState(*scratch)
    feature = lax.broadcasted_iota(jnp.int32, (dk, 1), 0)
    for s in range(subtiles):
        q = qt_ref[s]
        zero = jnp.zeros_like(q)
        st.qt[:, 2 * s * tk:(2 * s + 1) * tk] = jnp.where(feature < DIFF_HEAD_DIM, q, zero)
        st.qt[:, (2 * s + 1) * tk:(2 * s + 2) * tk] = jnp.where(feature >= DIFF_HEAD_DIM, q, zero)
    ot = _causal_attention(st, k_ref, vt_ref, i, tk, 2)

    lq = lam_ref[...]
    lam = (jnp.exp(jnp.sum(lq[0:1] * lq[1:2], axis=-1, keepdims=True))
           - jnp.exp(jnp.sum(lq[2:3] * lq[3:4], axis=-1, keepdims=True)) + LAMBDA_INIT)
    for s in range(subtiles):
        c = 2 * s * tk
        od = ot[:, c:c + tk] - lam * ot[:, c + tk:c + 2 * tk]
        od = od * lax.rsqrt(jnp.mean(od * od, axis=0, keepdims=True) + RMS_EPS)
        o_ref[s * tk:(s + 1) * tk, :] = (od.T * (g_ref[...] * (1.0 - LAMBDA_INIT))).astype(BF16)


def _mla_attn_kernel(qt_ref, k_ref, vt_ref, o_ref, *scratch):
    i = pl.program_id(2)
    subtiles, _, tk = qt_ref.shape
    st = _AttnState(*scratch)
    for s in range(subtiles):
        st.qt[:, s * tk:(s + 1) * tk] = qt_ref[s]
    ot = _causal_attention(st, k_ref, vt_ref, i, tk, 1)
    for s in range(subtiles):
        o_ref[s * tk:(s + 1) * tk, :] = ot[:, s * tk:(s + 1) * tk].T.astype(BF16)


def _attn_call(body, name, qt, k, vt, extra, *, maps, columns):
    B, H, nc, dk, tk = qt.shape
    S = k.shape[2]
    dv = vt.shape[3]
    subtiles = min(columns // (maps * tk), nc)
    tq = subtiles * tk
    assert S % tq == 0
    return pl.pallas_call(
        body,
        grid=(B, H, S // tq),
        in_specs=[
            pl.BlockSpec((None, None, subtiles, dk, tk), lambda b, h, i: (b, h, i, 0, 0)),
            pl.BlockSpec((None, None, S, dk), lambda b, h, i: (b, h, 0, 0)),
            pl.BlockSpec((None, None, nc, dv, tk), lambda b, h, i: (b, h, 0, 0, 0)),
        ] + [pl.BlockSpec(e.shape, lambda b, h, i: (0, 0)) for e in extra],
        out_specs=pl.BlockSpec((None, None, tq, dv), lambda b, h, i: (b, h, i, 0)),
        out_shape=jax.ShapeDtypeStruct((B, H, S, dv), BF16),
        scratch_shapes=_AttnState.scratch_shapes(dk, dv, tk, maps * tq),
        compiler_params=pltpu.CompilerParams(
            dimension_semantics=("parallel", "parallel", "arbitrary"), vmem_limit_bytes=VMEM_LIMIT_BYTES),
        name=name,
    )(qt, k, vt, *extra)


def _out_kernel(x_ref, lng_ref, lnb_ref, oa_ref, za_ref, ob_ref, zb_ref, g_ref, p_ref,
                woa_ref, wob_ref, wout_ref, wpg_ref, bpg_ref, wpp_ref, pog_ref, pob_ref, y_ref):
    for r0 in range(0, x_ref.shape[0], MERGE_SUBTILE):
        rows = slice(r0, r0 + MERGE_SUBTILE)
        xn = _layer_norm(x_ref[rows, :], lng_ref[...], lnb_ref[...])
        oa = jnp.concatenate([oa_ref[h, rows, :] for h in range(DIFF_HEADS)], axis=-1)
        ob = jnp.concatenate([ob_ref[h, rows, :] for h in range(MLA_HEADS)], axis=-1)
        ya = _dot(oa * za_ref[rows, :], woa_ref[...])
        yb = _dot(ob * zb_ref[rows, :], wob_ref[...])
        g = g_ref[rows, :]
        merged = g[:, :D_MODEL].astype(F32) * ya + g[:, D_MODEL:].astype(F32) * yb
        y = DEEPNORM_ALPHA * xn + _dot(merged.astype(BF16), wout_ref[...])
        gate = jax.nn.sigmoid(_dot(y.astype(BF16), wpg_ref[...]) + bpg_ref[...])
        y = y + gate * _dot(p_ref[rows, :].astype(BF16), wpp_ref[...])
        y_ref[rows, :] = _layer_norm(y, pog_ref[...], pob_ref[...])


def _out_call(x, lng, lnb, oa, za, ob, zb, g, p, woa, wob, wout, wpg, bpg, wpp, pog, pob, *, tm):
    B, S, D = x.shape
    row = lambda b, i: (b, i, 0)
    head = lambda b, i: (b, 0, i, 0)
    in_specs = [
        pl.BlockSpec((None, tm, D), row), _resident(lng.shape), _resident(lnb.shape),
        pl.BlockSpec((None, DIFF_HEADS, tm, HEAD_LANES), head),
        pl.BlockSpec((None, tm, DIFF_WIDTH), row),
        pl.BlockSpec((None, MLA_HEADS, tm, MLA_V), head),
        pl.BlockSpec((None, tm, MLA_WIDTH), row),
        pl.BlockSpec((None, tm, 2 * D), row),
        pl.BlockSpec((None, tm, PLE_DIM), row),
        _resident(woa.shape), _resident(wob.shape), _resident(wout.shape), _resident(wpg.shape),
        _resident(bpg.shape), _resident(wpp.shape), _resident(pog.shape), _resident(pob.shape),
    ]
    return pl.pallas_call(
        _out_kernel, grid=(B, S // tm), in_specs=in_specs,
        out_specs=pl.BlockSpec((None, tm, D), row),
        out_shape=jax.ShapeDtypeStruct((B, S, D), F32),
        compiler_params=pltpu.CompilerParams(
            dimension_semantics=("parallel", "parallel"), vmem_limit_bytes=VMEM_LIMIT_BYTES),
        name="merge_out",
    )(x, lng, lnb, oa, za, ob, zb, g, p, woa, wob, wout, wpg, bpg, wpp, pog, pob)


WIN_LAYOUT_ROWS = 128

W_KA = DIFF_QK_WIDTH
W_VA = W_KA + DIFF_QK_WIDTH
W_ZA = W_VA + DIFF_WIDTH
W_KPE = W_ZA + DIFF_WIDTH + MLA_Q_LORA + MLA_KV_LORA
W_ZB = W_KPE + MLA_ROPE


def _win_layout_kernel(wt_ref, win_ref, wqat_ref, wvat_ref):
    def put(off, lo, hi):
        win_ref[:, off:off + hi - lo] = wt_ref[lo:hi, :].T.astype(BF16)

    wqat_ref[...] = wt_ref[0:W_KA, :].astype(BF16)
    wvat_ref[...] = wt_ref[W_VA:W_ZA, :].astype(BF16)
    put(OFF_KA, W_KA, W_VA)
    put(OFF_ZA, W_ZA, W_KPE)
    put(OFF_KPE, W_KPE, W_ZB)
    win_ref[:, OFF_KPE + MLA_ROPE:OFF_ZB] = jnp.zeros((win_ref.shape[0], LANES - MLA_ROPE), BF16)
    put(OFF_ZB, W_ZB, wt_ref.shape[0])


def _win_layout_call(w_t):
    n_in, D = w_t.shape
    rows = WIN_LAYOUT_ROWS
    n_pad = n_in - DIFF_QK_WIDTH - DIFF_WIDTH + LANES - MLA_ROPE
    return pl.pallas_call(
        _win_layout_kernel, grid=(D // rows,),
        in_specs=[pl.BlockSpec((n_in, rows), lambda i: (0, i))],
        out_specs=[pl.BlockSpec((rows, n_pad), lambda i: (i, 0)),
                   pl.BlockSpec((DIFF_QK_WIDTH, rows), lambda i: (0, i)),
                   pl.BlockSpec((DIFF_WIDTH, rows), lambda i: (0, i))],
        out_shape=[jax.ShapeDtypeStruct((D, n_pad), BF16), jax.ShapeDtypeStruct((DIFF_QK_WIDTH, D), BF16),
                   jax.ShapeDtypeStruct((DIFF_WIDTH, D), BF16)],
        compiler_params=pltpu.CompilerParams(
            dimension_semantics=("parallel",), vmem_limit_bytes=VMEM_LIMIT_BYTES),
        name="win_layout",
    )(w_t)


def kernel(x, p, positions, ln_emb_g, ln_emb_b, w_in, b_gate, diff_lambda, diff_subln_g, w_o_a,
           mla_q_norm_g, mla_w_uq, mla_kv_norm_g, mla_w_ukv, w_o_b, w_out, ple_w_gate, ple_b_gate,
           ple_w_proj, ln_post_g, ln_post_b):
    B, S, D = x.shape
    assert D == D_MODEL and w_in.shape[0] == DEPTH == 1
    tk = min(KEY_CHUNK, S)
    tm = min(PROJ_ROWS, S)
    assert S % tm == 0
    row2 = lambda v: v.reshape(1, -1)

    win, wqat, wvat = _win_layout_call(w_in[0].T)
    wuq = mla_w_uq[0].astype(BF16).reshape(MLA_Q_LORA, MLA_HEADS, MLA_NOPE + MLA_ROPE)
    wuqt = jnp.concatenate([wuq[:, :, :MLA_NOPE].reshape(MLA_Q_LORA, -1),
                            wuq[:, :, MLA_NOPE:].reshape(MLA_Q_LORA, -1)], axis=1).T
    wukv = mla_w_ukv[0].astype(BF16).reshape(MLA_KV_LORA, MLA_HEADS, MLA_NOPE + MLA_V)
    wuk = wukv[:, :, :MLA_NOPE].reshape(MLA_KV_LORA, -1)
    wuvt = wukv[:, :, MLA_NOPE:].reshape(MLA_KV_LORA, -1).T
    inv = ROPE_THETA ** (-jnp.arange(0, DIFF_HEAD_DIM, 2, dtype=F32) / DIFF_HEAD_DIM)
    inv = jnp.tile(inv, LANES // inv.shape[0]).reshape(1, LANES)

    consts = (inv, row2(ln_emb_g), row2(ln_emb_b), win, wqat, wvat, row2(b_gate[0]),
              row2(mla_q_norm_g[0]), wuqt, row2(mla_kv_norm_g[0]), wuk, wuvt)
    qat, ka, vat, za, qbt, kb, vbt, zb, g = _proj_call(x, positions.reshape(B, S, 1), consts, tm=tm, tk=tk)
    oa = _attn_call(_diff_attn_kernel, "diff_attn", qat, ka, vat,
                    (diff_lambda[0], row2(diff_subln_g[0])), maps=2, columns=DIFF_ATTN_COLUMNS)
    ob = _attn_call(_mla_attn_kernel, "mla_attn", qbt, kb, vbt, (), maps=1, columns=MLA_ATTN_COLUMNS)
    return _out_call(
        x, row2(ln_emb_g), row2(ln_emb_b), oa, za, ob, zb, g, p[0],
        w_o_a[0].astype(BF16), w_o_b[0].astype(BF16), w_out[0].astype(BF16), ple_w_gate[0].astype(BF16),
        row2(ple_b_gate[0]), ple_w_proj[0].astype(BF16), row2(ln_post_g[0]), row2(ln_post_b[0]), tm=min(MERGE_ROWS, S))
```

```python
import functools
import math

import jax
import jax.numpy as jnp
from jax import lax
from jax.experimental import pallas as pl
from jax.experimental.pallas import tpu as pltpu

F32 = jnp.float32
BF16 = jnp.bfloat16

D_MODEL = 1024
DEPTH = 1
PLE_DIM = 256
ROPE_THETA = 10000.0
LN_EPS = 1e-5
RMS_EPS = 1e-6
NEG_INF = -1e30

DIFF_HEADS = 8
DIFF_HEAD_DIM = 64
DIFF_QK_WIDTH = 2 * DIFF_HEADS * DIFF_HEAD_DIM
DIFF_WIDTH = DIFF_HEADS * 2 * DIFF_HEAD_DIM

MLA_HEADS = 8
MLA_Q_LORA = 384
MLA_KV_LORA = 256
MLA_NOPE = 128
MLA_ROPE = 64
MLA_V = 128
MLA_WIDTH = MLA_HEADS * MLA_V

DEEPNORM_ALPHA = (2 * DEPTH) ** 0.25
LAMBDA_INIT = 0.8 - 0.6 * math.exp(-0.3 * 0)

LANES = 128
HEAD_LANES = 2 * DIFF_HEAD_DIM
MLA_QK_PAD = 256
LOG2E = math.log2(math.e)
HALF_ROT = DIFF_HEAD_DIM // 2

KEY_CHUNK = 256
DIFF_ATTN_COLUMNS = 4096
MLA_ATTN_COLUMNS = 2048
ATTN_COLUMN_BLOCK = 512
DIFF_ATTN_PAD_TILES = (1, 0)
MLA_ATTN_PAD_TILES = (1, 1)
PROJ_ROWS = KEY_CHUNK
MERGE_SUBTILE = 256
MERGE_ROWS = 2 * MERGE_SUBTILE

OFF_KA = 0
OFF_ZA = OFF_KA + DIFF_QK_WIDTH
OFF_CQ = OFF_ZA + DIFF_WIDTH
OFF_CKV = OFF_CQ + MLA_Q_LORA
OFF_KPE = OFF_CKV + MLA_KV_LORA
OFF_ZB = OFF_KPE + LANES
OFF_G = OFF_ZB + MLA_WIDTH

VMEM_LIMIT_BYTES = 56 * 1024 * 1024


def _layer_norm(x, g, b):
    mu = jnp.mean(x, axis=-1, keepdims=True)
    xc = x - mu
    var = jnp.mean(xc * xc, axis=-1, keepdims=True)
    return xc * lax.rsqrt(var + LN_EPS) * g + b


def _rms_norm(x, g):
    return x * lax.rsqrt(jnp.mean(x * x, axis=-1, keepdims=True) + RMS_EPS) * g


def _dot(a, b):
    return jnp.dot(a, b, preferred_element_type=F32)


def _dot_nt(a, b):
    return lax.dot_general(a, b, (((1,), (1,)), ((), ())), preferred_element_type=F32)


def _proj_kernel(x_ref, pos_ref, inv_ref, lng_ref, lnb_ref, win_ref, wqat_ref, wvat_ref, bg_ref,
                 qng_ref, wuqt_ref, kvg_ref, wuk_ref, wuvt_ref,
                 qat_ref, ka_ref, vat_ref, za_ref, qbt_ref, kb_ref, vbt_ref, zb_ref, g_ref):
    tk = qat_ref.shape[-1]
    for r in range(x_ref.shape[0] // tk):
        _proj_rows(r, slice(r * tk, (r + 1) * tk), x_ref, pos_ref, inv_ref, lng_ref, lnb_ref, win_ref,
                   wqat_ref, wvat_ref, bg_ref, qng_ref, wuqt_ref, kvg_ref, wuk_ref, wuvt_ref,
                   qat_ref, ka_ref, vat_ref, za_ref, qbt_ref, kb_ref, vbt_ref, zb_ref, g_ref)


def _proj_rows(r, rows, x_ref, pos_ref, inv_ref, lng_ref, lnb_ref, win_ref, wqat_ref, wvat_ref, bg_ref,
               qng_ref, wuqt_ref, kvg_ref, wuk_ref, wuvt_ref,
               qat_ref, ka_ref, vat_ref, za_ref, qbt_ref, kb_ref, vbt_ref, zb_ref, g_ref):
    xn = _layer_norm(x_ref[rows, :], lng_ref[...], lnb_ref[...]).astype(BF16)

    lane = lax.broadcasted_iota(jnp.int32, (1, LANES), 1)
    group = lane // HALF_ROT
    ngroups = LANES // HALF_ROT
    quarter = (rows.stop - rows.start) // ngroups
    pos = pos_ref[rows, :].astype(F32)
    ang = pos[0:quarter] * inv_ref[...]
    for j in range(1, ngroups):
        ang = jnp.where(group == j, pos[j * quarter:(j + 1) * quarter] * inv_ref[...], ang)

    def replicate(compact):
        parts = []
        for j in range(ngroups):
            one = jnp.where(group == j, compact, 0.0)
            two = one + pltpu.roll(one, 2 * HALF_ROT, 1)
            parts.append(two + pltpu.roll(two, HALF_ROT, 1))
        return jnp.concatenate(parts, axis=0)

    cos = replicate(jnp.cos(ang))
    sin = replicate(jnp.sin(ang))
    first_half = (lane % DIFF_HEAD_DIM) < HALF_ROT
    sin_signed = jnp.where(first_half, -sin, sin)
    cos_t = cos.T
    sin_signed_t = sin_signed.T

    def rope(t):
        partner = jnp.where(first_half, pltpu.roll(t, LANES - HALF_ROT, 1), pltpu.roll(t, HALF_ROT, 1))
        return t * cos + partner * sin_signed

    def rope_t(t):
        h = HALF_ROT
        halves = [t[j * h:(j + 1) * h] for j in range(t.shape[0] // h)]
        partner = jnp.concatenate([halves[j ^ 1] for j in range(len(halves))], axis=0)
        return t * cos_t[:t.shape[0]] + partner * sin_signed_t[:t.shape[0]]

    def proj(off, width):
        return _dot(xn, win_ref[:, off:off + width])

    g_ref[rows, :] = jax.nn.sigmoid(proj(OFF_G, 2 * D_MODEL) + bg_ref[...]).astype(BF16)
    hzb = proj(OFF_ZB, MLA_WIDTH)
    zb_ref[rows, :] = (hzb * jax.nn.sigmoid(hzb)).astype(BF16)
    hz = proj(OFF_ZA, DIFF_WIDTH)
    za_ref[rows, :] = (hz * jax.nn.sigmoid(hz)).astype(BF16)
    hvt = _dot_nt(wvat_ref[...], xn)
    for h in range(DIFF_HEADS):
        vat_ref[h, r] = hvt[h * HEAD_LANES:(h + 1) * HEAD_LANES].astype(BF16)

    latent = proj(OFF_CQ, OFF_ZB - OFF_CQ)
    cq = _rms_norm(latent[:, :MLA_Q_LORA], qng_ref[...]).astype(BF16)
    ckv = _rms_norm(latent[:, MLA_Q_LORA:MLA_Q_LORA + MLA_KV_LORA], kvg_ref[...]).astype(BF16)
    uvt = _dot_nt(wuvt_ref[...], ckv)
    for h in range(MLA_HEADS):
        vbt_ref[h, r] = uvt[h * MLA_V:(h + 1) * MLA_V].astype(BF16)

    qa_scale = DIFF_HEAD_DIM ** -0.5 * LOG2E
    hqt = _dot_nt(wqat_ref[...], xn)
    for h in range(DIFF_HEADS):
        qat_ref[h, r] = (rope_t(hqt[h * HEAD_LANES:(h + 1) * HEAD_LANES]) * qa_scale).astype(BF16)
    hk = proj(OFF_KA, DIFF_QK_WIDTH)
    for h in range(DIFF_HEADS):
        ka_ref[h, rows, :] = rope(hk[:, h * HEAD_LANES:(h + 1) * HEAD_LANES]).astype(BF16)

    qb_scale = (MLA_NOPE + MLA_ROPE) ** -0.5 * LOG2E
    uqt = _dot_nt(wuqt_ref[...], cq)
    pe0 = MLA_HEADS * MLA_NOPE
    pe_end = MLA_NOPE + MLA_ROPE
    for h in range(MLA_HEADS):
        qbt_ref[h, r, 0:MLA_NOPE, :] = (uqt[h * MLA_NOPE:(h + 1) * MLA_NOPE] * qb_scale).astype(BF16)
        qbt_ref[h, r, MLA_NOPE:pe_end, :] = (
            rope_t(uqt[pe0 + h * MLA_ROPE:pe0 + (h + 1) * MLA_ROPE]) * qb_scale).astype(BF16)
        qbt_ref[h, r, pe_end:MLA_QK_PAD, :] = jnp.zeros((MLA_QK_PAD - pe_end, uqt.shape[1]), BF16)

    k_rot = rope(latent[:, MLA_Q_LORA + MLA_KV_LORA:]).astype(BF16)
    uk = _dot(ckv, wuk_ref[...])
    for h in range(MLA_HEADS):
        kb_ref[h, rows, 0:LANES] = uk[:, h * MLA_NOPE:(h + 1) * MLA_NOPE].astype(BF16)
        kb_ref[h, rows, LANES:2 * LANES] = k_rot


def _resident(shape):
    nd = len(shape)
    return pl.BlockSpec(shape, lambda *_: (0,) * nd, pipeline_mode=pl.Buffered(1))


def _proj_call(x, pos, consts, *, tm, tk):
    B, S, D = x.shape
    grid = (B, S // tm)
    row = lambda b, i: (b, i, 0)
    head = lambda b, i: (b, 0, i, 0)
    head_t = lambda b, i: (b, 0, i, 0, 0)
    bf = lambda shape: jax.ShapeDtypeStruct(shape, BF16)
    nc = S // tk
    out_shape = [
        bf((B, DIFF_HEADS, nc, HEAD_LANES, tk)), bf((B, DIFF_HEADS, S, HEAD_LANES)),
        bf((B, DIFF_HEADS, nc, HEAD_LANES, tk)), bf((B, S, DIFF_WIDTH)),
        bf((B, MLA_HEADS, nc, MLA_QK_PAD, tk)), bf((B, MLA_HEADS, S, MLA_QK_PAD)),
        bf((B, MLA_HEADS, nc, MLA_V, tk)), bf((B, S, MLA_WIDTH)), bf((B, S, 2 * D)),
    ]
    out_specs = [
        pl.BlockSpec((None, DIFF_HEADS, tm // tk, HEAD_LANES, tk), head_t),
        pl.BlockSpec((None, DIFF_HEADS, tm, HEAD_LANES), head),
        pl.BlockSpec((None, DIFF_HEADS, tm // tk, HEAD_LANES, tk), head_t),
        pl.BlockSpec((None, tm, DIFF_WIDTH), row),
        pl.BlockSpec((None, MLA_HEADS, tm // tk, MLA_QK_PAD, tk), head_t),
        pl.BlockSpec((None, MLA_HEADS, tm, MLA_QK_PAD), head),
        pl.BlockSpec((None, MLA_HEADS, tm // tk, MLA_V, tk), head_t),
        pl.BlockSpec((None, tm, MLA_WIDTH), row),
        pl.BlockSpec((None, tm, 2 * D), row),
    ]
    in_specs = [pl.BlockSpec((None, tm, D), row), pl.BlockSpec((None, tm, 1), row)]
    in_specs += [_resident(c.shape) for c in consts]
    return pl.pallas_call(
        _proj_kernel, grid=grid, in_specs=in_specs, out_specs=out_specs, out_shape=out_shape,
        compiler_params=pltpu.CompilerParams(
            dimension_semantics=("parallel", "parallel"), vmem_limit_bytes=VMEM_LIMIT_BYTES),
        name="proj",
    )(x, pos, *consts)


SUM_ROWS = 16


class _AttnState:
    def __init__(self, qt_ref, s_ref, cmax_ref, m_ref, acc_ref):
        self.qt = qt_ref
        self.s = s_ref
        self.cmax = cmax_ref
        self.m = m_ref
        self.acc = acc_ref

    @staticmethod
    def scratch_shapes(dk, dv, tk, nq, pad_tiles):
        score_pad, acc_pad = pad_tiles
        return [pltpu.VMEM((dk, nq), BF16), pltpu.VMEM((2, tk, nq + score_pad * LANES), F32),
                pltpu.VMEM((2, 1, nq), F32), pltpu.VMEM((1, nq), F32),
                pltpu.VMEM((dv + SUM_ROWS, nq + acc_pad * LANES), F32)]


def _causal_attention(st, k_ref, vt_ref, i, tk, maps):
    nq = st.qt.shape[1]
    dv = vt_ref.shape[1]
    group = maps * tk
    subtiles = nq // group
    assert subtiles % 2 == 0 or subtiles == 1
    n = i * subtiles
    ones = jnp.ones((SUM_ROWS, tk), BF16)
    visible = (lax.broadcasted_iota(jnp.int32, (tk, tk), 0) <= lax.broadcasted_iota(jnp.int32, (tk, tk), 1))

    def store_scores(slot, c0, s, boundary):
        if boundary:
            s = jnp.where(jnp.concatenate([visible] * (s.shape[1] // tk), axis=1), s, NEG_INF)
        st.s[slot, :, c0:c0 + s.shape[1]] = s
        st.cmax[slot, :, c0:c0 + s.shape[1]] = jnp.max(s, axis=0, keepdims=True)

    def column_blocks(c0):
        return [(a, min(a + ATTN_COLUMN_BLOCK, nq)) for a in range(c0, nq, ATTN_COLUMN_BLOCK)]

    def scores(c, slot, c0=0, diagonal=False):
        k = k_ref[pl.ds(pl.multiple_of(c * tk, tk), tk), :]
        for a, b in column_blocks(c0):
            s = _dot(k, st.qt[:, a:b])
            edge = max(0, min(b, c0 + group) - a) if diagonal else 0
            if edge > 0:
                store_scores(slot, a, s[:, :edge], True)
            if edge < b - a:
                store_scores(slot, a + edge, s[:, edge:], False)

    def softmax_pv(c, slot, c0=0):
        values = jnp.concatenate([vt_ref[c], ones], axis=0)
        for a, b in column_blocks(c0):
            m_prev = st.m[:, a:b]
            m_new = jnp.maximum(m_prev, st.cmax[slot, :, a:b])
            pt = jnp.exp2(st.s[slot, :, a:b] - m_new).astype(BF16)
            st.m[:, a:b] = m_new
            st.acc[:, a:b] = jnp.exp2(m_prev - m_new) * st.acc[:, a:b] + _dot(values, pt)

    scores(0, 0)
    st.acc[...] = jnp.zeros_like(st.acc)
    st.m[...] = jnp.full_like(st.m, NEG_INF)

    def key_tile(t, carry):
        for u in range(subtiles):
            c = t * subtiles + u
            scores(c + 1, (u + 1) % 2)
            softmax_pv(c, u % 2)
        return carry

    lax.fori_loop(0, i, key_tile, 0)

    store_scores(0, 0, st.s[0, :, 0:group], True)
    for r in range(1, subtiles):
        scores(n + r, r % 2, r * group, diagonal=True)
        softmax_pv(n + r - 1, (r - 1) % 2, (r - 1) * group)
    softmax_pv(n + subtiles - 1, (subtiles - 1) % 2, (subtiles - 1) * group)
    return st.acc[0:dv, 0:nq] * (1.0 / st.acc[dv:dv + 1, 0:nq])


def _diff_attn_kernel(qt_ref, k_ref, vt_ref, lam_ref, g_ref, o_ref, *scratch):
    i = pl.program_id(2)
    subtiles, dk, tk = qt_ref.shape
    st = _AttnState(*scratch)
    feature = lax.broadcasted_iota(jnp.int32, (dk, 1), 0)
    for s in range(subtiles):
        q = qt_ref[s]
        zero = jnp.zeros_like(q)
        st.qt[:, 2 * s * tk:(2 * s + 1) * tk] = jnp.where(feature < DIFF_HEAD_DIM, q, zero)
        st.qt[:, (2 * s + 1) * tk:(2 * s + 2) * tk] = jnp.where(feature >= DIFF_HEAD_DIM, q, zero)
    ot = _causal_attention(st, k_ref, vt_ref, i, tk, 2)

    lq = lam_ref[...]
    lam = (jnp.exp(jnp.sum(lq[0:1] * lq[1:2], axis=-1, keepdims=True))
           - jnp.exp(jnp.sum(lq[2:3] * lq[3:4], axis=-1, keepdims=True)) + LAMBDA_INIT)
    for s in range(subtiles):
        c = 2 * s * tk
        od = ot[:, c:c + tk] - lam * ot[:, c + tk:c + 2 * tk]
        od = od * lax.rsqrt(jnp.mean(od * od, axis=0, keepdims=True) + RMS_EPS)
        o_ref[s * tk:(s + 1) * tk, :] = (od.T * (g_ref[...] * (1.0 - LAMBDA_INIT))).astype(BF16)


def _mla_attn_kernel(qt_ref, k_ref, vt_ref, o_ref, *scratch):
    i = pl.program_id(2)
    subtiles, _, tk = qt_ref.shape
    st = _AttnState(*scratch)
    for s in range(subtiles):
        st.qt[:, s * tk:(s + 1) * tk] = qt_ref[s]
    ot = _causal_attention(st, k_ref, vt_ref, i, tk, 1)
    for s in range(subtiles):
        o_ref[s * tk:(s + 1) * tk, :] = ot[:, s * tk:(s + 1) * tk].T.astype(BF16)


def _attn_call(body, name, qt, k, vt, extra, *, maps, columns, pad_tiles):
    B, H, nc, dk, tk = qt.shape
    S = k.shape[2]
    dv = vt.shape[3]
    subtiles = min(columns // (maps * tk), nc)
    tq = subtiles * tk
    assert S % tq == 0
    return pl.pallas_call(
        body,
        grid=(B, H, S // tq),
        in_specs=[
            pl.BlockSpec((None, None, subtiles, dk, tk), lambda b, h, i: (b, h, i, 0, 0)),
            pl.BlockSpec((None, None, S, dk), lambda b, h, i: (b, h, 0, 0)),
            pl.BlockSpec((None, None, nc, dv, tk), lambda b, h, i: (b, h, 0, 0, 0)),
        ] + [pl.BlockSpec(e.shape, lambda b, h, i: (0, 0)) for e in extra],
        out_specs=pl.BlockSpec((None, None, tq, dv), lambda b, h, i: (b, h, i, 0)),
        out_shape=jax.ShapeDtypeStruct((B, H, S, dv), BF16),
        scratch_shapes=_AttnState.scratch_shapes(dk, dv, tk, maps * tq, pad_tiles),
        compiler_params=pltpu.CompilerParams(
            dimension_semantics=("parallel", "parallel", "arbitrary"), vmem_limit_bytes=VMEM_LIMIT_BYTES),
        name=name,
    )(qt, k, vt, *extra)


def _out_kernel(x_ref, lng_ref, lnb_ref, oa_ref, za_ref, ob_ref, zb_ref, g_ref, p_ref,
                woa_ref, wob_ref, wout_ref, wpg_ref, bpg_ref, wpp_ref, pog_ref, pob_ref, y_ref):
    for r0 in range(0, x_ref.shape[0], MERGE_SUBTILE):
        rows = slice(r0, r0 + MERGE_SUBTILE)
        xn = _layer_norm(x_ref[rows, :], lng_ref[...], lnb_ref[...])
        oa = jnp.concatenate([oa_ref[h, rows, :] for h in range(DIFF_HEADS)], axis=-1)
        ob = jnp.concatenate([ob_ref[h, rows, :] for h in range(MLA_HEADS)], axis=-1)
        ya = _dot(oa * za_ref[rows, :], woa_ref[...])
        yb = _dot(ob * zb_ref[rows, :], wob_ref[...])
        g = g_ref[rows, :]
        merged = g[:, :D_MODEL].astype(F32) * ya + g[:, D_MODEL:].astype(F32) * yb
        y = DEEPNORM_ALPHA * xn + _dot(merged.astype(BF16), wout_ref[...])
        gate = jax.nn.sigmoid(_dot(y.astype(BF16), wpg_ref[...]) + bpg_ref[...])
        y = y + gate * _dot(p_ref[rows, :].astype(BF16), wpp_ref[...])
        y_ref[rows, :] = _layer_norm(y, pog_ref[...], pob_ref[...])


def _out_call(x, lng, lnb, oa, za, ob, zb, g, p, woa, wob, wout, wpg, bpg, wpp, pog, pob, *, tm):
    B, S, D = x.shape
    row = lambda b, i: (b, i, 0)
    head = lambda b, i: (b, 0, i, 0)
    in_specs = [
        pl.BlockSpec((None, tm, D), row), _resident(lng.shape), _resident(lnb.shape),
        pl.BlockSpec((None, DIFF_HEADS, tm, HEAD_LANES), head),
        pl.BlockSpec((None, tm, DIFF_WIDTH), row),
        pl.BlockSpec((None, MLA_HEADS, tm, MLA_V), head),
        pl.BlockSpec((None, tm, MLA_WIDTH), row),
        pl.BlockSpec((None, tm, 2 * D), row),
        pl.BlockSpec((None, tm, PLE_DIM), row),
        _resident(woa.shape), _resident(wob.shape), _resident(wout.shape), _resident(wpg.shape),
        _resident(bpg.shape), _resident(wpp.shape), _resident(pog.shape), _resident(pob.shape),
    ]
    return pl.pallas_call(
        _out_kernel, grid=(B, S // tm), in_specs=in_specs,
        out_specs=pl.BlockSpec((None, tm, D), row),
        out_shape=jax.ShapeDtypeStruct((B, S, D), F32),
        compiler_params=pltpu.CompilerParams(
            dimension_semantics=("parallel", "parallel"), vmem_limit_bytes=VMEM_LIMIT_BYTES),
        name="merge_out",
    )(x, lng, lnb, oa, za, ob, zb, g, p, woa, wob, wout, wpg, bpg, wpp, pog, pob)


WIN_LAYOUT_ROWS = 128

W_KA = DIFF_QK_WIDTH
W_VA = W_KA + DIFF_QK_WIDTH
W_ZA = W_VA + DIFF_WIDTH
W_KPE = W_ZA + DIFF_WIDTH + MLA_Q_LORA + MLA_KV_LORA
W_ZB = W_KPE + MLA_ROPE


def _win_layout_kernel(wt_ref, win_ref, wqat_ref, wvat_ref):
    def put(off, lo, hi):
        win_ref[:, off:off + hi - lo] = wt_ref[lo:hi, :].T.astype(BF16)

    wqat_ref[...] = wt_ref[0:W_KA, :].astype(BF16)
    wvat_ref[...] = wt_ref[W_VA:W_ZA, :].astype(BF16)
    put(OFF_KA, W_KA, W_VA)
    put(OFF_ZA, W_ZA, W_KPE)
    put(OFF_KPE, W_KPE, W_ZB)
    win_ref[:, OFF_KPE + MLA_ROPE:OFF_ZB] = jnp.zeros((win_ref.shape[0], LANES - MLA_ROPE), BF16)
    put(OFF_ZB, W_ZB, wt_ref.shape[0])


def _win_layout_call(w_t):
    n_in, D = w_t.shape
    rows = WIN_LAYOUT_ROWS
    n_pad = n_in - DIFF_QK_WIDTH - DIFF_WIDTH + LANES - MLA_ROPE
    return pl.pallas_call(
        _win_layout_kernel, grid=(D // rows,),
        in_specs=[pl.BlockSpec((n_in, rows), lambda i: (0, i))],
        out_specs=[pl.BlockSpec((rows, n_pad), lambda i: (i, 0)),
                   pl.BlockSpec((DIFF_QK_WIDTH, rows), lambda i: (0, i)),
                   pl.BlockSpec((DIFF_WIDTH, rows), lambda i: (0, i))],
        out_shape=[jax.ShapeDtypeStruct((D, n_pad), BF16), jax.ShapeDtypeStruct((DIFF_QK_WIDTH, D), BF16),
                   jax.ShapeDtypeStruct((DIFF_WIDTH, D), BF16)],
        compiler_params=pltpu.CompilerParams(
            dimension_semantics=("parallel",), vmem_limit_bytes=VMEM_LIMIT_BYTES),
        name="win_layout",
    )(w_t)


def kernel(x, p, positions, ln_emb_g, ln_emb_b, w_in, b_gate, diff_lambda, diff_subln_g, w_o_a,
           mla_q_norm_g, mla_w_uq, mla_kv_norm_g, mla_w_ukv, w_o_b, w_out, ple_w_gate, ple_b_gate,
           ple_w_proj, ln_post_g, ln_post_b):
    B, S, D = x.shape
    assert D == D_MODEL and w_in.shape[0] == DEPTH == 1
    tk = min(KEY_CHUNK, S)
    tm = min(PROJ_ROWS, S)
    assert S % tm == 0
    row2 = lambda v: v.reshape(1, -1)

    win, wqat, wvat = _win_layout_call(w_in[0].T)
    wuq = mla_w_uq[0].astype(BF16).reshape(MLA_Q_LORA, MLA_HEADS, MLA_NOPE + MLA_ROPE)
    wuqt = jnp.concatenate([wuq[:, :, :MLA_NOPE].reshape(MLA_Q_LORA, -1),
                            wuq[:, :, MLA_NOPE:].reshape(MLA_Q_LORA, -1)], axis=1).T
    wukv = mla_w_ukv[0].astype(BF16).reshape(MLA_KV_LORA, MLA_HEADS, MLA_NOPE + MLA_V)
    wuk = wukv[:, :, :MLA_NOPE].reshape(MLA_KV_LORA, -1)
    wuvt = wukv[:, :, MLA_NOPE:].reshape(MLA_KV_LORA, -1).T
    inv = ROPE_THETA ** (-jnp.arange(0, DIFF_HEAD_DIM, 2, dtype=F32) / DIFF_HEAD_DIM)
    inv = jnp.tile(inv, LANES // inv.shape[0]).reshape(1, LANES)

    consts = (inv, row2(ln_emb_g), row2(ln_emb_b), win, wqat, wvat, row2(b_gate[0]),
              row2(mla_q_norm_g[0]), wuqt, row2(mla_kv_norm_g[0]), wuk, wuvt)
    qat, ka, vat, za, qbt, kb, vbt, zb, g = _proj_call(x, positions.reshape(B, S, 1), consts, tm=tm, tk=tk)
    oa = _attn_call(_diff_attn_kernel, "diff_attn", qat, ka, vat,
                    (diff_lambda[0], row2(diff_subln_g[0])), maps=2, columns=DIFF_ATTN_COLUMNS,
                    pad_tiles=DIFF_ATTN_PAD_TILES)
    ob = _attn_call(_mla_attn_kernel, "mla_attn", qbt, kb, vbt, (), maps=1, columns=MLA_ATTN_COLUMNS,
                    pad_tiles=MLA_ATTN_PAD_TILES)
    return _out_call(
        x, row2(ln_emb_g), row2(ln_emb_b), oa, za, ob, zb, g, p[0],
        w_o_a[0].astype(BF16), w_o_b[0].astype(BF16), w_out[0].astype(BF16), ple_w_gate[0].astype(BF16),
        row2(ple_b_gate[0]), ple_w_proj[0].astype(BF16), row2(ln_post_g[0]), row2(ln_post_b[0]), tm=min(MERGE_ROWS, S))
```

```python
import functools
import math

import jax
import jax.numpy as jnp
from jax import lax
from jax.experimental import pallas as pl
from jax.experimental.pallas import tpu as pltpu

F32 = jnp.float32
BF16 = jnp.bfloat16

D_MODEL = 1024
DEPTH = 1
PLE_DIM = 256
ROPE_THETA = 10000.0
LN_EPS = 1e-5
RMS_EPS = 1e-6
NEG_INF = -1e30

DIFF_HEADS = 8
DIFF_HEAD_DIM = 64
DIFF_QK_WIDTH = 2 * DIFF_HEADS * DIFF_HEAD_DIM
DIFF_WIDTH = DIFF_HEADS * 2 * DIFF_HEAD_DIM

MLA_HEADS = 8
MLA_Q_LORA = 384
MLA_KV_LORA = 256
MLA_NOPE = 128
MLA_ROPE = 64
MLA_V = 128
MLA_WIDTH = MLA_HEADS * MLA_V

DEEPNORM_ALPHA = (2 * DEPTH) ** 0.25
LAMBDA_INIT = 0.8 - 0.6 * math.exp(-0.3 * 0)

LANES = 128
HEAD_LANES = 2 * DIFF_HEAD_DIM
MLA_QK_PAD = 256
LOG2E = math.log2(math.e)
HALF_ROT = DIFF_HEAD_DIM // 2

KEY_CHUNK = 256
DIFF_ATTN_COLUMNS = 4096
MLA_ATTN_COLUMNS = 2048
ATTN_COLUMN_BLOCK = 512
DIFF_ATTN_PAD_TILES = (1, 1, 0)
MLA_ATTN_PAD_TILES = (1, 1, 1)
PROJ_ROWS = KEY_CHUNK
MERGE_SUBTILE = 256
MERGE_ROWS = 2 * MERGE_SUBTILE

OFF_KA = 0
OFF_ZA = OFF_KA + DIFF_QK_WIDTH
OFF_CQ = OFF_ZA + DIFF_WIDTH
OFF_CKV = OFF_CQ + MLA_Q_LORA
OFF_KPE = OFF_CKV + MLA_KV_LORA
OFF_ZB = OFF_KPE + LANES
OFF_G = OFF_ZB + MLA_WIDTH

VMEM_LIMIT_BYTES = 56 * 1024 * 1024


def _layer_norm(x, g, b):
    mu = jnp.mean(x, axis=-1, keepdims=True)
    xc = x - mu
    var = jnp.mean(xc * xc, axis=-1, keepdims=True)
    return xc * lax.rsqrt(var + LN_EPS) * g + b


def _rms_norm(x, g):
    return x * lax.rsqrt(jnp.mean(x * x, axis=-1, keepdims=True) + RMS_EPS) * g


def _dot(a, b):
    return jnp.dot(a, b, preferred_element_type=F32)


def _dot_nt(a, b):
    return lax.dot_general(a, b, (((1,), (1,)), ((), ())), preferred_element_type=F32)


def _proj_kernel(x_ref, pos_ref, inv_ref, lng_ref, lnb_ref, win_ref, wqat_ref, wvat_ref, bg_ref,
                 qng_ref, wuqt_ref, kvg_ref, wuk_ref, wuvt_ref,
                 qat_ref, ka_ref, vat_ref, za_ref, qbt_ref, kb_ref, vbt_ref, zb_ref, g_ref):
    tk = qat_ref.shape[-1]
    for r in range(x_ref.shape[0] // tk):
        _proj_rows(r, slice(r * tk, (r + 1) * tk), x_ref, pos_ref, inv_ref, lng_ref, lnb_ref, win_ref,
                   wqat_ref, wvat_ref, bg_ref, qng_ref, wuqt_ref, kvg_ref, wuk_ref, wuvt_ref,
                   qat_ref, ka_ref, vat_ref, za_ref, qbt_ref, kb_ref, vbt_ref, zb_ref, g_ref)


def _proj_rows(r, rows, x_ref, pos_ref, inv_ref, lng_ref, lnb_ref, win_ref, wqat_ref, wvat_ref, bg_ref,
               qng_ref, wuqt_ref, kvg_ref, wuk_ref, wuvt_ref,
               qat_ref, ka_ref, vat_ref, za_ref, qbt_ref, kb_ref, vbt_ref, zb_ref, g_ref):
    xn = _layer_norm(x_ref[rows, :], lng_ref[...], lnb_ref[...]).astype(BF16)

    lane = lax.broadcasted_iota(jnp.int32, (1, LANES), 1)
    group = lane // HALF_ROT
    ngroups = LANES // HALF_ROT
    quarter = (rows.stop - rows.start) // ngroups
    pos = pos_ref[rows, :].astype(F32)
    ang = pos[0:quarter] * inv_ref[...]
    for j in range(1, ngroups):
        ang = jnp.where(group == j, pos[j * quarter:(j + 1) * quarter] * inv_ref[...], ang)

    def replicate(compact):
        parts = []
        for j in range(ngroups):
            one = jnp.where(group == j, compact, 0.0)
            two = one + pltpu.roll(one, 2 * HALF_ROT, 1)
            parts.append(two + pltpu.roll(two, HALF_ROT, 1))
        return jnp.concatenate(parts, axis=0)

    cos = replicate(jnp.cos(ang))
    sin = replicate(jnp.sin(ang))
    first_half = (lane % DIFF_HEAD_DIM) < HALF_ROT
    sin_signed = jnp.where(first_half, -sin, sin)
    cos_t = cos.T
    sin_signed_t = sin_signed.T

    def rope(t):
        partner = jnp.where(first_half, pltpu.roll(t, LANES - HALF_ROT, 1), pltpu.roll(t, HALF_ROT, 1))
        return t * cos + partner * sin_signed

    def rope_t(t):
        h = HALF_ROT
        halves = [t[j * h:(j + 1) * h] for j in range(t.shape[0] // h)]
        partner = jnp.concatenate([halves[j ^ 1] for j in range(len(halves))], axis=0)
        return t * cos_t[:t.shape[0]] + partner * sin_signed_t[:t.shape[0]]

    def proj(off, width):
        return _dot(xn, win_ref[:, off:off + width])

    g_ref[rows, :] = jax.nn.sigmoid(proj(OFF_G, 2 * D_MODEL) + bg_ref[...]).astype(BF16)
    hzb = proj(OFF_ZB, MLA_WIDTH)
    zb_ref[rows, :] = (hzb * jax.nn.sigmoid(hzb)).astype(BF16)
    hz = proj(OFF_ZA, DIFF_WIDTH)
    za_ref[rows, :] = (hz * jax.nn.sigmoid(hz)).astype(BF16)
    hvt = _dot_nt(wvat_ref[...], xn)
    for h in range(DIFF_HEADS):
        vat_ref[h, r] = hvt[h * HEAD_LANES:(h + 1) * HEAD_LANES].astype(BF16)

    latent = proj(OFF_CQ, OFF_ZB - OFF_CQ)
    cq = _rms_norm(latent[:, :MLA_Q_LORA], qng_ref[...]).astype(BF16)
    ckv = _rms_norm(latent[:, MLA_Q_LORA:MLA_Q_LORA + MLA_KV_LORA], kvg_ref[...]).astype(BF16)
    uvt = _dot_nt(wuvt_ref[...], ckv)
    for h in range(MLA_HEADS):
        vbt_ref[h, r] = uvt[h * MLA_V:(h + 1) * MLA_V].astype(BF16)

    qa_scale = DIFF_HEAD_DIM ** -0.5 * LOG2E
    hqt = _dot_nt(wqat_ref[...], xn)
    for h in range(DIFF_HEADS):
        qat_ref[h, r] = (rope_t(hqt[h * HEAD_LANES:(h + 1) * HEAD_LANES]) * qa_scale).astype(BF16)
    hk = proj(OFF_KA, DIFF_QK_WIDTH)
    for h in range(DIFF_HEADS):
        ka_ref[h, rows, :] = rope(hk[:, h * HEAD_LANES:(h + 1) * HEAD_LANES]).astype(BF16)

    qb_scale = (MLA_NOPE + MLA_ROPE) ** -0.5 * LOG2E
    uqt = _dot_nt(wuqt_ref[...], cq)
    pe0 = MLA_HEADS * MLA_NOPE
    pe_end = MLA_NOPE + MLA_ROPE
    for h in range(MLA_HEADS):
        qbt_ref[h, r, 0:MLA_NOPE, :] = (uqt[h * MLA_NOPE:(h + 1) * MLA_NOPE] * qb_scale).astype(BF16)
        qbt_ref[h, r, MLA_NOPE:pe_end, :] = (
            rope_t(uqt[pe0 + h * MLA_ROPE:pe0 + (h + 1) * MLA_ROPE]) * qb_scale).astype(BF16)
        qbt_ref[h, r, pe_end:MLA_QK_PAD, :] = jnp.zeros((MLA_QK_PAD - pe_end, uqt.shape[1]), BF16)

    k_rot = rope(latent[:, MLA_Q_LORA + MLA_KV_LORA:]).astype(BF16)
    uk = _dot(ckv, wuk_ref[...])
    for h in range(MLA_HEADS):
        kb_ref[h, rows, 0:LANES] = uk[:, h * MLA_NOPE:(h + 1) * MLA_NOPE].astype(BF16)
        kb_ref[h, rows, LANES:2 * LANES] = k_rot


def _resident(shape):
    nd = len(shape)
    return pl.BlockSpec(shape, lambda *_: (0,) * nd, pipeline_mode=pl.Buffered(1))


def _proj_call(x, pos, consts, *, tm, tk):
    B, S, D = x.shape
    grid = (B, S // tm)
    row = lambda b, i: (b, i, 0)
    head = lambda b, i: (b, 0, i, 0)
    head_t = lambda b, i: (b, 0, i, 0, 0)
    bf = lambda shape: jax.ShapeDtypeStruct(shape, BF16)
    nc = S // tk
    out_shape = [
        bf((B, DIFF_HEADS, nc, HEAD_LANES, tk)), bf((B, DIFF_HEADS, S, HEAD_LANES)),
        bf((B, DIFF_HEADS, nc, HEAD_LANES, tk)), bf((B, S, DIFF_WIDTH)),
        bf((B, MLA_HEADS, nc, MLA_QK_PAD, tk)), bf((B, MLA_HEADS, S, MLA_QK_PAD)),
        bf((B, MLA_HEADS, nc, MLA_V, tk)), bf((B, S, MLA_WIDTH)), bf((B, S, 2 * D)),
    ]
    out_specs = [
        pl.BlockSpec((None, DIFF_HEADS, tm // tk, HEAD_LANES, tk), head_t),
        pl.BlockSpec((None, DIFF_HEADS, tm, HEAD_LANES), head),
        pl.BlockSpec((None, DIFF_HEADS, tm // tk, HEAD_LANES, tk), head_t),
        pl.BlockSpec((None, tm, DIFF_WIDTH), row),
        pl.BlockSpec((None, MLA_HEADS, tm // tk, MLA_QK_PAD, tk), head_t),
        pl.BlockSpec((None, MLA_HEADS, tm, MLA_QK_PAD), head),
        pl.BlockSpec((None, MLA_HEADS, tm // tk, MLA_V, tk), head_t),
        pl.BlockSpec((None, tm, MLA_WIDTH), row),
        pl.BlockSpec((None, tm, 2 * D), row),
    ]
    in_specs = [pl.BlockSpec((None, tm, D), row), pl.BlockSpec((None, tm, 1), row)]
    in_specs += [_resident(c.shape) for c in consts]
    return pl.pallas_call(
        _proj_kernel, grid=grid, in_specs=in_specs, out_specs=out_specs, out_shape=out_shape,
        compiler_params=pltpu.CompilerParams(
            dimension_semantics=("parallel", "parallel"), vmem_limit_bytes=VMEM_LIMIT_BYTES),
        name="proj",
    )(x, pos, *consts)


SUM_ROWS = 16


class _AttnState:
    def __init__(self, qt_ref, s_ref, cmax_ref, m_ref, acc_ref):
        self.qt = qt_ref
        self.s = s_ref
        self.cmax = cmax_ref
        self.m = m_ref
        self.acc = acc_ref

    @staticmethod
    def scratch_shapes(dk, dv, tk, nq, pad_tiles):
        query_pad, score_pad, acc_pad = pad_tiles
        return [pltpu.VMEM((dk, nq + query_pad * LANES), BF16), pltpu.VMEM((2, tk, nq + score_pad * LANES), F32),
                pltpu.VMEM((2, 1, nq), F32), pltpu.VMEM((1, nq), F32),
                pltpu.VMEM((dv + SUM_ROWS, nq + acc_pad * LANES), F32)]


def _causal_attention(st, k_ref, vt_ref, i, tk, maps, nq):
    dv = vt_ref.shape[1]
    group = maps * tk
    subtiles = nq // group
    assert subtiles % 2 == 0 or subtiles == 1
    n = i * subtiles
    ones = jnp.ones((SUM_ROWS, tk), BF16)
    visible = (lax.broadcasted_iota(jnp.int32, (tk, tk), 0) <= lax.broadcasted_iota(jnp.int32, (tk, tk), 1))

    def store_scores(slot, c0, s, boundary):
        if boundary:
            s = jnp.where(jnp.concatenate([visible] * (s.shape[1] // tk), axis=1), s, NEG_INF)
        st.s[slot, :, c0:c0 + s.shape[1]] = s
        st.cmax[slot, :, c0:c0 + s.shape[1]] = jnp.max(s, axis=0, keepdims=True)

    def column_blocks(c0):
        return [(a, min(a + ATTN_COLUMN_BLOCK, nq)) for a in range(c0, nq, ATTN_COLUMN_BLOCK)]

    def scores(c, slot, c0=0, diagonal=False):
        k = k_ref[pl.ds(pl.multiple_of(c * tk, tk), tk), :]
        for a, b in column_blocks(c0):
            s = _dot(k, st.qt[:, a:b])
            edge = max(0, min(b, c0 + group) - a) if diagonal else 0
            if edge > 0:
                store_scores(slot, a, s[:, :edge], True)
            if edge < b - a:
                store_scores(slot, a + edge, s[:, edge:], False)

    def softmax_pv(c, slot, c0=0):
        values = jnp.concatenate([vt_ref[c], ones], axis=0)
        for a, b in column_blocks(c0):
            m_prev = st.m[:, a:b]
            m_new = jnp.maximum(m_prev, st.cmax[slot, :, a:b])
            pt = jnp.exp2(st.s[slot, :, a:b] - m_new).astype(BF16)
            st.m[:, a:b] = m_new
            st.acc[:, a:b] = jnp.exp2(m_prev - m_new) * st.acc[:, a:b] + _dot(values, pt)

    scores(0, 0)
    st.acc[...] = jnp.zeros_like(st.acc)
    st.m[...] = jnp.full_like(st.m, NEG_INF)

    def key_tile(t, carry):
        for u in range(subtiles):
            c = t * subtiles + u
            scores(c + 1, (u + 1) % 2)
            softmax_pv(c, u % 2)
        return carry

    lax.fori_loop(0, i, key_tile, 0)

    store_scores(0, 0, st.s[0, :, 0:group], True)
    for r in range(1, subtiles):
        scores(n + r, r % 2, r * group, diagonal=True)
        softmax_pv(n + r - 1, (r - 1) % 2, (r - 1) * group)
    softmax_pv(n + subtiles - 1, (subtiles - 1) % 2, (subtiles - 1) * group)
    return st.acc[0:dv, 0:nq] * (1.0 / st.acc[dv:dv + 1, 0:nq])


def _diff_attn_kernel(qt_ref, k_ref, vt_ref, lam_ref, g_ref, o_ref, *scratch):
    i = pl.program_id(2)
    subtiles, dk, tk = qt_ref.shape
    st = _AttnState(*scratch)
    feature = lax.broadcasted_iota(jnp.int32, (dk, 1), 0)
    for s in range(subtiles):
        q = qt_ref[s]
        zero = jnp.zeros_like(q)
        st.qt[:, 2 * s * tk:(2 * s + 1) * tk] = jnp.where(feature < DIFF_HEAD_DIM, q, zero)
        st.qt[:, (2 * s + 1) * tk:(2 * s + 2) * tk] = jnp.where(feature >= DIFF_HEAD_DIM, q, zero)
    ot = _causal_attention(st, k_ref, vt_ref, i, tk, 2, 2 * subtiles * tk)

    lq = lam_ref[...]
    lam = (jnp.exp(jnp.sum(lq[0:1] * lq[1:2], axis=-1, keepdims=True))
           - jnp.exp(jnp.sum(lq[2:3] * lq[3:4], axis=-1, keepdims=True)) + LAMBDA_INIT)
    for s in range(subtiles):
        c = 2 * s * tk
        od = ot[:, c:c + tk] - lam * ot[:, c + tk:c + 2 * tk]
        od = od * lax.rsqrt(jnp.mean(od * od, axis=0, keepdims=True) + RMS_EPS)
        o_ref[s * tk:(s + 1) * tk, :] = (od.T * (g_ref[...] * (1.0 - LAMBDA_INIT))).astype(BF16)


def _mla_attn_kernel(qt_ref, k_ref, vt_ref, o_ref, *scratch):
    i = pl.program_id(2)
    subtiles, _, tk = qt_ref.shape
    st = _AttnState(*scratch)
    for s in range(subtiles):
        st.qt[:, s * tk:(s + 1) * tk] = qt_ref[s]
    ot = _causal_attention(st, k_ref, vt_ref, i, tk, 1, subtiles * tk)
    for s in range(subtiles):
        o_ref[s * tk:(s + 1) * tk, :] = ot[:, s * tk:(s + 1) * tk].T.astype(BF16)


def _attn_call(body, name, qt, k, vt, extra, *, maps, columns, pad_tiles):
    B, H, nc, dk, tk = qt.shape
    S = k.shape[2]
    dv = vt.shape[3]
    subtiles = min(columns // (maps * tk), nc)
    tq = subtiles * tk
    assert S % tq == 0
    return pl.pallas_call(
        body,
        grid=(B, H, S // tq),
        in_specs=[
            pl.BlockSpec((None, None, subtiles, dk, tk), lambda b, h, i: (b, h, i, 0, 0)),
            pl.BlockSpec((None, None, S, dk), lambda b, h, i: (b, h, 0, 0)),
            pl.BlockSpec((None, None, nc, dv, tk), lambda b, h, i: (b, h, 0, 0, 0)),
        ] + [pl.BlockSpec(e.shape, lambda b, h, i: (0, 0)) for e in extra],
        out_specs=pl.BlockSpec((None, None, tq, dv), lambda b, h, i: (b, h, i, 0)),
        out_shape=jax.ShapeDtypeStruct((B, H, S, dv), BF16),
        scratch_shapes=_AttnState.scratch_shapes(dk, dv, tk, maps * tq, pad_tiles),
        compiler_params=pltpu.CompilerParams(
            dimension_semantics=("parallel", "parallel", "arbitrary"), vmem_limit_bytes=VMEM_LIMIT_BYTES),
        name=name,
    )(qt, k, vt, *extra)


def _out_kernel(x_ref, lng_ref, lnb_ref, oa_ref, za_ref, ob_ref, zb_ref, g_ref, p_ref,
                woa_ref, wob_ref, wout_ref, wpg_ref, bpg_ref, wpp_ref, pog_ref, pob_ref, y_ref):
    for r0 in range(0, x_ref.shape[0], MERGE_SUBTILE):
        rows = slice(r0, r0 + MERGE_SUBTILE)
        xn = _layer_norm(x_ref[rows, :], lng_ref[...], lnb_ref[...])
        oa = jnp.concatenate([oa_ref[h, rows, :] for h in range(DIFF_HEADS)], axis=-1)
        ob = jnp.concatenate([ob_ref[h, rows, :] for h in range(MLA_HEADS)], axis=-1)
        ya = _dot(oa * za_ref[rows, :], woa_ref[...])
        yb = _dot(ob * zb_ref[rows, :], wob_ref[...])
        g = g_ref[rows, :]
        merged = g[:, :D_MODEL].astype(F32) * ya + g[:, D_MODEL:].astype(F32) * yb
        y = DEEPNORM_ALPHA * xn + _dot(merged.astype(BF16), wout_ref[...])
        gate = jax.nn.sigmoid(_dot(y.astype(BF16), wpg_ref[...]) + bpg_ref[...])
        y = y + gate * _dot(p_ref[rows, :].astype(BF16), wpp_ref[...])
        y_ref[rows, :] = _layer_norm(y, pog_ref[...], pob_ref[...])


def _out_call(x, lng, lnb, oa, za, ob, zb, g, p, woa, wob, wout, wpg, bpg, wpp, pog, pob, *, tm):
    B, S, D = x.shape
    row = lambda b, i: (b, i, 0)
    head = lambda b, i: (b, 0, i, 0)
    in_specs = [
        pl.BlockSpec((None, tm, D), row), _resident(lng.shape), _resident(lnb.shape),
        pl.BlockSpec((None, DIFF_HEADS, tm, HEAD_LANES), head),
        pl.BlockSpec((None, tm, DIFF_WIDTH), row),
        pl.BlockSpec((None, MLA_HEADS, tm, MLA_V), head),
        pl.BlockSpec((None, tm, MLA_WIDTH), row),
        pl.BlockSpec((None, tm, 2 * D), row),
        pl.BlockSpec((None, tm, PLE_DIM), row),
        _resident(woa.shape), _resident(wob.shape), _resident(wout.shape), _resident(wpg.shape),
        _resident(bpg.shape), _resident(wpp.shape), _resident(pog.shape), _resident(pob.shape),
    ]
    return pl.pallas_call(
        _out_kernel, grid=(B, S // tm), in_specs=in_specs,
        out_specs=pl.BlockSpec((None, tm, D), row),
        out_shape=jax.ShapeDtypeStruct((B, S, D), F32),
        compiler_params=pltpu.CompilerParams(
            dimension_semantics=("parallel", "parallel"), vmem_limit_bytes=VMEM_LIMIT_BYTES),
        name="merge_out",
    )(x, lng, lnb, oa, za, ob, zb, g, p, woa, wob, wout, wpg, bpg, wpp, pog, pob)


WIN_LAYOUT_ROWS = 128

W_KA = DIFF_QK_WIDTH
W_VA = W_KA + DIFF_QK_WIDTH
W_ZA = W_VA + DIFF_WIDTH
W_KPE = W_ZA + DIFF_WIDTH + MLA_Q_LORA + MLA_KV_LORA
W_ZB = W_KPE + MLA_ROPE


def _win_layout_kernel(wt_ref, win_ref, wqat_ref, wvat_ref):
    def put(off, lo, hi):
        win_ref[:, off:off + hi - lo] = wt_ref[lo:hi, :].T.astype(BF16)

    wqat_ref[...] = wt_ref[0:W_KA, :].astype(BF16)
    wvat_ref[...] = wt_ref[W_VA:W_ZA, :].astype(BF16)
    put(OFF_KA, W_KA, W_VA)
    put(OFF_ZA, W_ZA, W_KPE)
    put(OFF_KPE, W_KPE, W_ZB)
    win_ref[:, OFF_KPE + MLA_ROPE:OFF_ZB] = jnp.zeros((win_ref.shape[0], LANES - MLA_ROPE), BF16)
    put(OFF_ZB, W_ZB, wt_ref.shape[0])


def _win_layout_call(w_t):
    n_in, D = w_t.shape
    rows = WIN_LAYOUT_ROWS
    n_pad = n_in - DIFF_QK_WIDTH - DIFF_WIDTH + LANES - MLA_ROPE
    return pl.pallas_call(
        _win_layout_kernel, grid=(D // rows,),
        in_specs=[pl.BlockSpec((n_in, rows), lambda i: (0, i))],
        out_specs=[pl.BlockSpec((rows, n_pad), lambda i: (i, 0)),
                   pl.BlockSpec((DIFF_QK_WIDTH, rows), lambda i: (0, i)),
                   pl.BlockSpec((DIFF_WIDTH, rows), lambda i: (0, i))],
        out_shape=[jax.ShapeDtypeStruct((D, n_pad), BF16), jax.ShapeDtypeStruct((DIFF_QK_WIDTH, D), BF16),
                   jax.ShapeDtypeStruct((DIFF_WIDTH, D), BF16)],
        compiler_params=pltpu.CompilerParams(
            dimension_semantics=("parallel",), vmem_limit_bytes=VMEM_LIMIT_BYTES),
        name="win_layout",
    )(w_t)


def kernel(x, p, positions, ln_emb_g, ln_emb_b, w_in, b_gate, diff_lambda, diff_subln_g, w_o_a,
           mla_q_norm_g, mla_w_uq, mla_kv_norm_g, mla_w_ukv, w_o_b, w_out, ple_w_gate, ple_b_gate,
           ple_w_proj, ln_post_g, ln_post_b):
    B, S, D = x.shape
    assert D == D_MODEL and w_in.shape[0] == DEPTH == 1
    tk = min(KEY_CHUNK, S)
    tm = min(PROJ_ROWS, S)
    assert S % tm == 0
    row2 = lambda v: v.reshape(1, -1)

    win, wqat, wvat = _win_layout_call(w_in[0].T)
    wuq = mla_w_uq[0].astype(BF16).reshape(MLA_Q_LORA, MLA_HEADS, MLA_NOPE + MLA_ROPE)
    wuqt = jnp.concatenate([wuq[:, :, :MLA_NOPE].reshape(MLA_Q_LORA, -1),
                            wuq[:, :, MLA_NOPE:].reshape(MLA_Q_LORA, -1)], axis=1).T
    wukv = mla_w_ukv[0].astype(BF16).reshape(MLA_KV_LORA, MLA_HEADS, MLA_NOPE + MLA_V)
    wuk = wukv[:, :, :MLA_NOPE].reshape(MLA_KV_LORA, -1)
    wuvt = wukv[:, :, MLA_NOPE:].reshape(MLA_KV_LORA, -1).T
    inv = ROPE_THETA ** (-jnp.arange(0, DIFF_HEAD_DIM, 2, dtype=F32) / DIFF_HEAD_DIM)
    inv = jnp.tile(inv, LANES // inv.shape[0]).reshape(1, LANES)

    consts = (inv, row2(ln_emb_g), row2(ln_emb_b), win, wqat, wvat, row2(b_gate[0]),
              row2(mla_q_norm_g[0]), wuqt, row2(mla_kv_norm_g[0]), wuk, wuvt)
    qat, ka, vat, za, qbt, kb, vbt, zb, g = _proj_call(x, positions.reshape(B, S, 1), consts, tm=tm, tk=tk)
    oa = _attn_call(_diff_attn_kernel, "diff_attn", qat, ka, vat,
                    (diff_lambda[0], row2(diff_subln_g[0])), maps=2, columns=DIFF_ATTN_COLUMNS,
                    pad_tiles=DIFF_ATTN_PAD_TILES)
    ob = _attn_call(_mla_attn_kernel, "mla_attn", qbt, kb, vbt, (), maps=1, columns=MLA_ATTN_COLUMNS,
                    pad_tiles=MLA_ATTN_PAD_TILES)
    return _out_call(
        x, row2(ln_emb_g), row2(ln_emb_b), oa, za, ob, zb, g, p[0],
        w_o_a[0].astype(BF16), w_o_b[0].astype(BF16), w_out[0].astype(BF16), ple_w_gate[0].astype(BF16),
        row2(ple_b_gate[0]), ple_w_proj[0].astype(BF16), row2(ln_post_g[0]), row2(ln_post_b[0]), tm=min(MERGE_ROWS, S))
```
